```python
import math
import jax, jax.numpy as jnp
from jax import lax
import numpy as np

D_MODEL = 2048
BATCH = 2
SEQ = 8192
DEPTH = 2

EXPAND = 2
MIX_WIDTH = EXPAND * D_MODEL
N_EVEN = (DEPTH + 1) // 2
N_ODD = DEPTH // 2
RET_WIDTH = MIX_WIDTH // 2
RET_HEADS = 4
RET_DV = RET_WIDTH // RET_HEADS
RET_DK = RET_DV // 2
RET_QK_WIDTH = RET_HEADS * RET_DK
RET_CHUNK = 128
ROPE_BASE = 10000.0
SGU_WIDTH = MIX_WIDTH // 2
SGU_GROUPS = 8
SGU_GDIM = SGU_WIDTH // SGU_GROUPS
SGU_CHUNK = 128
GLA_WIDTH = MIX_WIDTH
GLA_HEADS = 8
GLA_DV = GLA_WIDTH // GLA_HEADS
GLA_DK = GLA_DV // 2
GLA_QK_WIDTH = GLA_HEADS * GLA_DK
GLA_RANK = 16
GLA_TAU = 16.0
GLA_CHUNK = 64
EPS = 1e-6

EVEN_SIZES = (RET_QK_WIDTH, RET_QK_WIDTH, RET_WIDTH, RET_WIDTH, SGU_WIDTH, SGU_WIDTH, SGU_WIDTH)
ODD_SIZES = (GLA_QK_WIDTH, GLA_QK_WIDTH, GLA_WIDTH, GLA_WIDTH, GLA_RANK)
EVEN_IN = sum(EVEN_SIZES)
ODD_IN = sum(ODD_SIZES)

kernel_name = "hybrid_retention_sgu_gla_trunk"


def split_cols(p, sizes):
    outs, off = [], 0
    for s in sizes:
        outs.append(p[..., off:off + s])
        off += s
    return outs


def rms_norm(x, g):
    xf = x.astype(jnp.float32)
    y = xf * lax.rsqrt(jnp.mean(xf * xf, axis=-1, keepdims=True) + EPS)
    return (y * g.astype(jnp.float32)).astype(x.dtype)


def head_norm(o):
    of = o.astype(jnp.float32)
    return (of * lax.rsqrt(jnp.mean(of * of, axis=-1, keepdims=True) + EPS)).astype(o.dtype)


def rotary(x, positions):
    half = x.shape[-1] // 2
    inv = jnp.power(ROPE_BASE, -jnp.arange(half, dtype=jnp.float32) / half)
    ang = positions.astype(jnp.float32)[:, :, None] * inv
    cos = jnp.cos(ang)[:, :, None, :]
    sin = jnp.sin(ang)[:, :, None, :]
    xf = x.astype(jnp.float32)
    x1, x2 = xf[..., :half], xf[..., half:]
    return jnp.concatenate([x1 * cos - x2 * sin, x1 * sin + x2 * cos], axis=-1).astype(x.dtype)


def to_chunks(x, c):
    b, s, h, d = x.shape
    return x.reshape(b, s // c, c, h, d).transpose(1, 0, 3, 2, 4)


def from_chunks(x):
    nc, b, h, c, d = x.shape
    return x.transpose(1, 0, 3, 2, 4).reshape(b, nc * c, h, d)


def retention(q, k, v, positions):
    b, s, h, dk = q.shape
    dv = v.shape[-1]
    c = RET_CHUNK
    q = rotary(q, positions) * (dk ** -0.5)
    k = rotary(k, positions)
    log_gamma = jnp.log1p(-jnp.power(2.0, -5.0 - jnp.arange(h, dtype=jnp.float32)))
    idx = jnp.arange(c, dtype=jnp.float32)
    rel = idx[:, None] - idx[None, :]
    causal = rel >= 0
    decay_intra = jnp.where(causal, jnp.exp(log_gamma[:, None, None] * jnp.where(causal, rel, 0.0)), 0.0)
    q_decay = jnp.exp(log_gamma[:, None] * (idx + 1.0))[:, :, None]
    k_decay = jnp.exp(log_gamma[:, None] * (c - 1.0 - idx))[:, :, None]
    chunk_decay = jnp.exp(log_gamma * c)[:, None, None]

    def step(state, inp):
        qi, ki, vi = (t.astype(jnp.float32) for t in inp)
        scores = jnp.einsum('bhtk,bhsk->bhts', qi, ki) * decay_intra
        o = (jnp.einsum('bhts,bhsv->bhtv', scores, vi)
             + jnp.einsum('bhtk,bhkv->bhtv', qi * q_decay, state))
        state = chunk_decay * state + jnp.einsum('bhsk,bhsv->bhkv', ki * k_decay, vi)
        return state, o

    state0 = jnp.zeros((b, h, dk, dv), jnp.float32)
    _, o = lax.scan(step, state0, (to_chunks(q, c), to_chunks(k, c), to_chunks(v, c)))
    return from_chunks(o).astype(v.dtype)


def spatial_gating(u, v, ln_gain, w_s, b_s):
    b, s, g, dg = v.shape
    c = SGU_CHUNK
    vf = v.astype(jnp.float32)
    mean = jnp.mean(vf, axis=-1, keepdims=True)
    var = jnp.mean(jnp.square(vf - mean), axis=-1, keepdims=True)
    vn = (vf - mean) * lax.rsqrt(var + EPS) * ln_gain.astype(jnp.float32).reshape(g, dg)
    vn = vn.reshape(b, s // c, c, g, dg)
    mask = jnp.tril(jnp.ones((c, c), dtype=bool))
    w = jnp.where(mask[None], w_s.astype(jnp.float32), 0.0)
    sg = jnp.einsum('gts,bnsgc->bntgc', w, vn) + b_s.astype(jnp.float32).T[None, None, :, :, None]
    return (u.astype(jnp.float32) * sg.reshape(b, s, g, dg)).astype(u.dtype)


def gla(q, k, v, log_alpha):
    b, s, h, dk = q.shape
    dv = v.shape[-1]
    c = GLA_CHUNK
    q = q * (dk ** -0.5)
    mask = jnp.tril(jnp.ones((c, c), dtype=bool))[:, :, None]

    def step(state, inp):
        qi, ki, vi, ai = (t.astype(jnp.float32) for t in inp)
        bc = jnp.cumsum(ai, axis=2)
        diff = bc[:, :, :, None, :] - bc[:, :, None, :, :]
        dec = jnp.exp(jnp.where(mask, diff, -jnp.inf))
        scores = jnp.einsum('bhtsk,bhsk->bhts', qi[:, :, :, None, :] * dec, ki)
        o = (jnp.einsum('bhts,bhsv->bhtv', scores, vi)
             + jnp.einsum('bhtk,bhkv->bhtv', qi * jnp.exp(bc), state))
        b_last = bc[:, :, -1:, :]
        state = (jnp.exp(b_last[:, :, 0, :])[..., None] * state
                 + jnp.einsum('bhsk,bhsv->bhkv', ki * jnp.exp(b_last - bc), vi))
        return state, o

    state0 = jnp.zeros((b, h, dk, dv), jnp.float32)
    xs = (to_chunks(q, c), to_chunks(k, c), to_chunks(v, c), to_chunks(log_alpha, c))
    _, o = lax.scan(step, state0, xs)
    return from_chunks(o).astype(v.dtype)


def even_layer(x, positions, norm_g, w_in, sgu_gain, sgu_w_s, sgu_b, w_out):
    b, s, _ = x.shape
    h = rms_norm(x, norm_g)
    proj = h @ w_in
    q, k, v, g_r, u, vs, g_s = split_cols(proj, EVEN_SIZES)
    ret = retention(q.reshape(b, s, RET_HEADS, RET_DK), k.reshape(b, s, RET_HEADS, RET_DK),
                    v.reshape(b, s, RET_HEADS, RET_DV), positions)
    ret = head_norm(ret).reshape(b, s, RET_WIDTH) * jax.nn.silu(g_r)
    sgu = spatial_gating(u.reshape(b, s, SGU_GROUPS, SGU_GDIM), vs.reshape(b, s, SGU_GROUPS, SGU_GDIM),
                         sgu_gain, sgu_w_s, sgu_b)
    sgu = sgu.reshape(b, s, SGU_WIDTH) * jax.nn.silu(g_s)
    mixed = jnp.concatenate([ret, sgu], axis=-1)
    return x + mixed @ w_out


def odd_layer(x, norm_g, w_in, w_lr, b_lr, w_out):
    b, s, _ = x.shape
    h = rms_norm(x, norm_g)
    proj = h @ w_in
    q, k, v, g, lr = split_cols(proj, ODD_SIZES)
    log_alpha = jax.nn.log_sigmoid((lr @ w_lr + b_lr).astype(jnp.float32)) / GLA_TAU
    o = gla(q.reshape(b, s, GLA_HEADS, GLA_DK), k.reshape(b, s, GLA_HEADS, GLA_DK),
            v.reshape(b, s, GLA_HEADS, GLA_DV), log_alpha.reshape(b, s, GLA_HEADS, GLA_DK))
    o = head_norm(o).reshape(b, s, GLA_WIDTH) * jax.nn.silu(g)
    return x + o @ w_out


def setup_inputs(seed: int = 0) -> dict:
    key = jax.random.key(seed)
    ks = jax.random.split(key, 16)
    nrm = jax.random.normal
    f32 = jnp.float32
    x = nrm(ks[0], (BATCH, SEQ, D_MODEL), f32)
    positions = jnp.tile(jnp.arange(SEQ, dtype=jnp.int32)[None, :], (BATCH, 1))
    ev_norm = 1.0 + 0.02 * nrm(ks[1], (N_EVEN, D_MODEL), f32)
    ev_w_in = nrm(ks[2], (N_EVEN, D_MODEL, EVEN_IN), f32) * D_MODEL ** -0.5
    sgu_gain = 1.0 + 0.02 * nrm(ks[3], (N_EVEN, SGU_WIDTH), f32)
    sgu_w_s = nrm(ks[4], (N_EVEN, SGU_GROUPS, SGU_CHUNK, SGU_CHUNK), f32) * SGU_CHUNK ** -0.5
    sgu_b = 1.0 + 0.02 * nrm(ks[5], (N_EVEN, SGU_GROUPS, SGU_CHUNK), f32)
    ev_w_out = nrm(ks[6], (N_EVEN, MIX_WIDTH, D_MODEL), f32) * MIX_WIDTH ** -0.5
    od_norm = 1.0 + 0.02 * nrm(ks[7], (N_ODD, D_MODEL), f32)
    od_w_in = nrm(ks[8], (N_ODD, D_MODEL, ODD_IN), f32) * D_MODEL ** -0.5
    gla_w_lr = nrm(ks[9], (N_ODD, GLA_RANK, GLA_QK_WIDTH), f32) * GLA_RANK ** -0.5
    gla_b_lr = 0.02 * nrm(ks[10], (N_ODD, GLA_QK_WIDTH), f32)
    od_w_out = nrm(ks[11], (N_ODD, MIX_WIDTH, D_MODEL), f32) * MIX_WIDTH ** -0.5
    final_norm = 1.0 + 0.02 * nrm(ks[12], (D_MODEL,), f32)
    return {"x": x, "positions": positions, "ev_norm": ev_norm, "ev_w_in": ev_w_in,
            "sgu_gain": sgu_gain, "sgu_w_s": sgu_w_s, "sgu_b": sgu_b, "ev_w_out": ev_w_out,
            "od_norm": od_norm, "od_w_in": od_w_in, "gla_w_lr": gla_w_lr, "gla_b_lr": gla_b_lr,
            "od_w_out": od_w_out, "final_norm": final_norm}


def reference(x, positions, ev_norm, ev_w_in, sgu_gain, sgu_w_s, sgu_b, ev_w_out,
              od_norm, od_w_in, gla_w_lr, gla_b_lr, od_w_out, final_norm):
    for i in range(DEPTH):
        j = i // 2
        if i % 2 == 0:
            x = even_layer(x, positions, ev_norm[j], ev_w_in[j], sgu_gain[j], sgu_w_s[j],
                           sgu_b[j], ev_w_out[j])
        else:
            x = odd_layer(x, od_norm[j], od_w_in[j], gla_w_lr[j], gla_b_lr[j], od_w_out[j])
    return rms_norm(x, final_norm)
```

```python
import functools

import numpy as np
import jax
import jax.numpy as jnp
from jax import lax
from jax.experimental import pallas as pl
from jax.experimental.pallas import tpu as pltpu

F32 = jnp.float32
BF16 = jnp.bfloat16

EPS = 1e-6
RET_HEADS = 4
RET_CHUNK = 128
ROPE_BASE = 10000.0
SGU_GROUPS = 8
SGU_CHUNK = 128
GLA_HEADS = 8
GLA_RANK = 16
GLA_TAU = 16.0
GLA_CHUNK = 64

LANES = 128
VMEM_LIMIT = 56 * 1024 * 1024

NORM_TM = 512
PROJ_TM = 1024
PROJ_TN = 1024
OUT_TM = 512
RET_TS = 512
SGU_TS = 512
GLA_TS = 256


def _params(*sem):
    return pltpu.CompilerParams(dimension_semantics=sem, vmem_limit_bytes=VMEM_LIMIT)


def _silu(g):
    return g * (1.0 / (1.0 + jnp.exp(-g)))


def _rms(x, gain):
    return x * lax.rsqrt(jnp.mean(x * x, axis=-1, keepdims=True) + EPS) * gain


def _rmsnorm_kernel(x_ref, g_ref, o_ref):
    o_ref[...] = _rms(x_ref[...], g_ref[...]).astype(o_ref.dtype)


def _rmsnorm(x, gain):
    t, d = x.shape
    tm = min(NORM_TM, t)
    return pl.pallas_call(
        _rmsnorm_kernel,
        out_shape=jax.ShapeDtypeStruct((t, d), BF16),
        grid=(t // tm,),
        in_specs=[pl.BlockSpec((tm, d), lambda i: (i, 0)),
                  pl.BlockSpec((1, d), lambda i: (0, 0))],
        out_specs=pl.BlockSpec((tm, d), lambda i: (i, 0)),
        compiler_params=_params("parallel"),
        name="rmsnorm",
    )(x, gain.reshape(1, d))


def _inproj_kernel(h_ref, w_ref, o_ref):
    o_ref[...] = jnp.dot(h_ref[...], w_ref[...], preferred_element_type=F32).astype(o_ref.dtype)


def _inproj_lr_kernel(h_ref, w_ref, wlr_ref, o_ref, lr_ref):
    o_ref[...] = jnp.dot(h_ref[...], w_ref[...], preferred_element_type=F32).astype(o_ref.dtype)

    @pl.when(pl.program_id(1) == 0)
    def _():
        lr_ref[...] = jnp.dot(h_ref[...], wlr_ref[...], preferred_element_type=F32)


def _inproj(h, w, wlr=None):
    t, d = h.shape
    n = w.shape[1]
    tm, tn = min(PROJ_TM, t), min(PROJ_TN, n)
    grid = (t // tm, n // tn)
    h_spec = pl.BlockSpec((tm, d), lambda i, j: (i, 0))
    w_spec = pl.BlockSpec((d, tn), lambda i, j: (0, j))
    o_spec = pl.BlockSpec((tm, tn), lambda i, j: (i, j))
    if wlr is None:
        return pl.pallas_call(
            _inproj_kernel,
            out_shape=jax.ShapeDtypeStruct((t, n), BF16),
            grid=grid, in_specs=[h_spec, w_spec], out_specs=o_spec,
            compiler_params=_params("parallel", "arbitrary"),
            name="inproj",
        )(h, w)
    return pl.pallas_call(
        _inproj_lr_kernel,
        out_shape=(jax.ShapeDtypeStruct((t, n), BF16), jax.ShapeDtypeStruct((t, LANES), F32)),
        grid=grid,
        in_specs=[h_spec, w_spec, pl.BlockSpec((d, LANES), lambda i, j: (0, 0))],
        out_specs=(o_spec, pl.BlockSpec((tm, LANES), lambda i, j: (i, 0))),
        compiler_params=_params("parallel", "arbitrary"),
        name="inproj_lr",
    )(h, w, wlr)


def _retention_tables(dk):
    c = RET_CHUNK
    h = np.arange(RET_HEADS, dtype=np.float64)
    log_gamma = np.log1p(-np.power(2.0, -5.0 - h))
    idx = np.arange(c, dtype=np.float64)
    rel = idx[:, None] - idx[None, :]
    dmat = np.where(rel >= 0, np.exp(log_gamma[:, None, None] * np.maximum(rel, 0.0)), 0.0)
    qdec = np.exp(log_gamma[:, None] * (idx + 1.0))
    kdec = np.exp(log_gamma[:, None] * (c - 1.0 - idx))
    cdec = np.exp(log_gamma * c)
    lanes = lambda a: np.broadcast_to(a[:, :, None], (RET_HEADS, c, LANES))
    return (jnp.asarray(dmat, F32), jnp.asarray(lanes(qdec), F32), jnp.asarray(lanes(kdec), F32),
            [float(v) for v in cdec], float(dk) ** -0.5)


def _retention_kernel(q_ref, k_ref, v_ref, g_ref, pos_ref, inv_ref, dmat_ref, qdec_ref, kdec_ref,
                      o_ref, state_ref, *, cdec, scale, dk, dv):
    c = RET_CHUNK
    half = dk // 2
    ts = q_ref.shape[1]

    @pl.when(pl.program_id(1) == 0)
    def _():
        state_ref[...] = jnp.zeros_like(state_ref)

    def chunk(ci, carry):
        rows = pl.ds(pl.multiple_of(ci * c, c), c)
        ang = pos_ref[0, rows, :] * inv_ref[...]
        cos, sin = jnp.cos(ang), jnp.sin(ang)
        cos_q, sin_q = cos * scale, sin * scale
        for h in range(RET_HEADS):
            q = q_ref[0, rows, h * dk:(h + 1) * dk].astype(F32)
            k = k_ref[0, rows, h * dk:(h + 1) * dk].astype(F32)
            q1, q2 = q[:, :half], q[:, half:]
            k1, k2 = k[:, :half], k[:, half:]
            qr1, qr2 = q1 * cos_q - q2 * sin_q, q1 * sin_q + q2 * cos_q
            kr1, kr2 = k1 * cos - k2 * sin, k1 * sin + k2 * cos
            qdec, kdec = qdec_ref[h], kdec_ref[h]
            qr = jnp.concatenate([qr1, qr2], axis=-1).astype(BF16)
            kr = jnp.concatenate([kr1, kr2], axis=-1).astype(BF16)
            qd = jnp.concatenate([qr1 * qdec, qr2 * qdec], axis=-1).astype(BF16)
            kd = jnp.concatenate([kr1 * kdec, kr2 * kdec], axis=-1).astype(BF16)
            s = lax.dot_general(qr, kr, (((1,), (1,)), ((), ())), preferred_element_type=F32)
            p = (s * dmat_ref[h]).astype(BF16)
            v = v_ref[0, rows, h * dv:(h + 1) * dv]
            st = state_ref[h]
            o = (jnp.dot(p, v, preferred_element_type=F32)
                 + jnp.dot(qd, st.astype(BF16), preferred_element_type=F32))
            state_ref[h] = cdec[h] * st + lax.dot_general(
                kd, v, (((0,), (0,)), ((), ())), preferred_element_type=F32)
            on = o * lax.rsqrt(jnp.mean(o * o, axis=-1, keepdims=True) + EPS)
            g = g_ref[0, rows, h * dv:(h + 1) * dv].astype(F32)
            o_ref[0, rows, h * dv:(h + 1) * dv] = (on * _silu(g)).astype(o_ref.dtype)
        return carry

    lax.fori_loop(0, ts // c, chunk, 0)


def _retention(proj, posf, inv):
    b, s, _ = proj.shape
    qk_w = proj.shape[2] // 12
    v_w = 2 * qk_w
    dk, dv = qk_w // RET_HEADS, v_w // RET_HEADS
    ts = min(RET_TS, s)
    dmat, qdec, kdec, cdec, scale = _retention_tables(dk)
    const3 = lambda shape: pl.BlockSpec(shape, lambda bi, i: (0, 0, 0))
    return pl.pallas_call(
        functools.partial(_retention_kernel, cdec=cdec, scale=scale, dk=dk, dv=dv),
        out_shape=jax.ShapeDtypeStruct((b, s, v_w), BF16),
        grid=(b, s // ts),
        in_specs=[pl.BlockSpec((1, ts, qk_w), lambda bi, i: (bi, i, 0)),
                  pl.BlockSpec((1, ts, qk_w), lambda bi, i: (bi, i, 1)),
                  pl.BlockSpec((1, ts, v_w), lambda bi, i: (bi, i, 1)),
                  pl.BlockSpec((1, ts, v_w), lambda bi, i: (bi, i, 2)),
                  pl.BlockSpec((1, ts, 1), lambda bi, i: (bi, i, 0)),
                  pl.BlockSpec((1, dk // 2), lambda bi, i: (0, 0)),
                  const3(dmat.shape), const3(qdec.shape), const3(kdec.shape)],
        out_specs=pl.BlockSpec((1, ts, v_w), lambda bi, i: (bi, i, 0)),
        scratch_shapes=[pltpu.VMEM((RET_HEADS, dk, dv), F32)],
        compiler_params=_params("parallel", "arbitrary"),
        name="retention",
    )(proj, proj, proj, proj, posf, inv, dmat, qdec, kdec)


def _sgu_kernel(u_ref, v_ref, g_ref, gain_ref, w_ref, b_ref, o_ref, *, dg):
    c = SGU_CHUNK
    ts = u_ref.shape[1]
    row = lax.broadcasted_iota(jnp.int32, (c, c), 0)
    col = lax.broadcasted_iota(jnp.int32, (c, c), 1)
    causal = row >= col
    wm = [jnp.where(causal, w_ref[g], 0.0).astype(BF16) for g in range(SGU_GROUPS)]

    def chunk(ci, carry):
        rows = pl.ds(pl.multiple_of(ci * c, c), c)
        for g in range(SGU_GROUPS):
            cols = slice(g * dg, (g + 1) * dg)
            v = v_ref[0, rows, cols].astype(F32)
            d = v - jnp.mean(v, axis=-1, keepdims=True)
            var = jnp.mean(d * d, axis=-1, keepdims=True)
            vn = d * lax.rsqrt(var + EPS) * gain_ref[:, cols]
            sg = jnp.dot(wm[g], vn.astype(BF16), preferred_element_type=F32) + b_ref[g]
            u = u_ref[0, rows, cols].astype(F32)
            gate = g_ref[0, rows, cols].astype(F32)
            o_ref[0, rows, cols] = (u * sg * _silu(gate)).astype(o_ref.dtype)
        return carry

    lax.fori_loop(0, ts // c, chunk, 0)


def _sgu(proj, gain, w_s, b_s):
    b, s, _ = proj.shape
    w = proj.shape[2] // 6
    dg = w // SGU_GROUPS
    ts = min(SGU_TS, s)
    bias = jnp.broadcast_to(b_s[:, :, None], (SGU_GROUPS, SGU_CHUNK, dg))
    const3 = lambda shape: pl.BlockSpec(shape, lambda bi, i: (0, 0, 0))
    return pl.pallas_call(
        functools.partial(_sgu_kernel, dg=dg),
        out_shape=jax.ShapeDtypeStruct((b, s, w), BF16),
        grid=(b, s // ts),
        in_specs=[pl.BlockSpec((1, ts, w), lambda bi, i: (bi, i, 3)),
                  pl.BlockSpec((1, ts, w), lambda bi, i: (bi, i, 4)),
                  pl.BlockSpec((1, ts, w), lambda bi, i: (bi, i, 5)),
                  pl.BlockSpec((1, w), lambda bi, i: (0, 0)),
                  const3(w_s.shape), const3(bias.shape)],
        out_specs=pl.BlockSpec((1, ts, w), lambda bi, i: (bi, i, 0)),
        compiler_params=_params("parallel", "parallel"),
        name="sgu",
    )(proj, proj, proj, gain.reshape(1, w), w_s, bias)


def _outproj_mid_kernel(a_ref, b_ref, x_ref, wa_ref, wb_ref, gn_ref, x1_ref, h1_ref):
    acc = (jnp.dot(a_ref[...], wa_ref[...], preferred_element_type=F32)
           + jnp.dot(b_ref[...], wb_ref[...], preferred_element_type=F32))
    x1 = x_ref[...] + acc
    x1_ref[...] = x1
    h1_ref[...] = _rms(x1, gn_ref[...]).astype(h1_ref.dtype)


def _outproj_final_kernel(a_ref, x_ref, w_ref, gn_ref, o_ref):
    x1 = x_ref[...] + jnp.dot(a_ref[...], w_ref[...], preferred_element_type=F32)
    o_ref[...] = _rms(x1, gn_ref[...])


def _resident(shape):
    return pl.BlockSpec(shape, lambda i: (0, 0), pipeline_mode=pl.Buffered(1))


def _outproj_mid(a, b, x, wa, wb, gain):
    t, d = x.shape
    ka, kb = a.shape[1], b.shape[1]
    tm = min(OUT_TM, t)
    row = lambda w: pl.BlockSpec((tm, w), lambda i: (i, 0))
    return pl.pallas_call(
        _outproj_mid_kernel,
        out_shape=(jax.ShapeDtypeStruct((t, d), F32), jax.ShapeDtypeStruct((t, d), BF16)),
        grid=(t // tm,),
        in_specs=[row(ka), row(kb), row(d), _resident((ka, d)), _resident((kb, d)), _resident((1, d))],
        out_specs=(row(d), row(d)),
        compiler_params=_params("parallel"),
        name="outproj_mid",
    )(a, b, x, wa, wb, gain.reshape(1, d))


def _outproj_final(a, x, w, gain):
    t, d = x.shape
    ka = a.shape[1]
    tm = min(OUT_TM, t)
    row = lambda w_: pl.BlockSpec((tm, w_), lambda i: (i, 0))
    return pl.pallas_call(
        _outproj_final_kernel,
        out_shape=jax.ShapeDtypeStruct((t, d), F32),
        grid=(t // tm,),
        in_specs=[row(ka), row(d), _resident((ka, d)), _resident((1, d))],
        out_specs=row(d),
        compiler_params=_params("parallel"),
        name="outproj_final",
    )(a, x, w, gain.reshape(1, d))


def _split_bf16(x):
    hi = x.astype(BF16)
    lo = (x - hi.astype(F32)).astype(BF16)
    return hi, lo


def _gla_kernel(q_ref, k_ref, v_ref, g_ref, lr_ref, wlr_ref, blr_ref, o_ref, state_ref, *, scale):
    c = GLA_CHUNK
    ts = q_ref.shape[1]
    dk = q_ref.shape[2]

    @pl.when(pl.program_id(2) == 0)
    def _():
        state_ref[...] = jnp.zeros_like(state_ref)

    lr_hi, lr_lo = _split_bf16(lr_ref[0])
    lane = lax.broadcasted_iota(jnp.int32, lr_hi.shape, 1)
    lr3 = jnp.where((lane >= GLA_RANK) & (lane < 2 * GLA_RANK), lr_lo, lr_hi)
    z = jnp.dot(lr3, wlr_ref[...], preferred_element_type=F32) + blr_ref[...]
    log_alpha = (jnp.minimum(z, 0.0) - jnp.log1p(jnp.exp(-jnp.abs(z)))) * (1.0 / GLA_TAU)

    row = lax.broadcasted_iota(jnp.int32, (c, c), 0)
    col = lax.broadcasted_iota(jnp.int32, (c, c), 1)
    causal = row >= col
    tri = jnp.where(causal, 1.0, 0.0).astype(BF16)
    tri2 = jnp.concatenate([tri, tri], axis=1)

    for ci in range(ts // c):
        rows = slice(ci * c, (ci + 1) * c)
        la_hi, la_lo = _split_bf16(log_alpha[rows])
        bc = jnp.dot(tri2, jnp.concatenate([la_hi, la_lo], axis=0), preferred_element_type=F32)
        mid = bc[c // 2 - 1:c // 2, :]
        last = bc[c - 1:c, :]
        q = q_ref[0, rows, :].astype(F32) * scale
        k = k_ref[0, rows, :].astype(F32)
        v = v_ref[0, rows, :]
        q_in = (q * jnp.exp(bc)).astype(BF16)
        q_t = (q * jnp.exp(bc - mid)).astype(BF16)
        k_t = (k * jnp.exp(mid - bc)).astype(BF16)
        k_out = (k * jnp.exp(last - bc)).astype(BF16)
        s = lax.dot_general(q_t, k_t, (((1,), (1,)), ((), ())), preferred_element_type=F32)
        p = jnp.where(causal, s, 0.0).astype(BF16)
        st = state_ref[...]
        o = (jnp.dot(p, v, preferred_element_type=F32)
             + lax.dot_general(q_in, st.astype(BF16), (((1,), (1,)), ((), ())),
                               preferred_element_type=F32))
        state_ref[...] = st * jnp.exp(last) + lax.dot_general(
            v, k_out, (((0,), (0,)), ((), ())), preferred_element_type=F32)
        on = o * lax.rsqrt(jnp.mean(o * o, axis=-1, keepdims=True) + EPS)
        g = g_ref[0, rows, :].astype(F32)
        o_ref[0, rows, :] = (on * _silu(g)).astype(o_ref.dtype)


def _gla(proj, lr, wlr3, blr):
    b, s, _ = proj.shape
    qk_w = proj.shape[2] // 6
    v_w = 2 * qk_w
    dk, dv = qk_w // GLA_HEADS, v_w // GLA_HEADS
    ts = min(GLA_TS, s)
    nh = GLA_HEADS
    return pl.pallas_call(
        functools.partial(_gla_kernel, scale=float(dk) ** -0.5),
        out_shape=jax.ShapeDtypeStruct((b, s, v_w), BF16),
        grid=(b, nh, s // ts),
        in_specs=[pl.BlockSpec((1, ts, dk), lambda bi, h, i: (bi, i, h)),
                  pl.BlockSpec((1, ts, dk), lambda bi, h, i: (bi, i, nh + h)),
                  pl.BlockSpec((1, ts, dv), lambda bi, h, i: (bi, i, nh + h)),
                  pl.BlockSpec((1, ts, dv), lambda bi, h, i: (bi, i, 2 * nh + h)),
                  pl.BlockSpec((1, ts, LANES), lambda bi, h, i: (bi, i, 0)),
                  pl.BlockSpec((LANES, dk), lambda bi, h, i: (0, h)),
                  pl.BlockSpec((1, dk), lambda bi, h, i: (0, h))],
        out_specs=pl.BlockSpec((1, ts, dv), lambda bi, h, i: (bi, i, h)),
        scratch_shapes=[pltpu.VMEM((dv, dk), F32)],
        compiler_params=_params("parallel", "parallel", "arbitrary"),
        name="gla",
    )(proj, proj, proj, proj, lr, wlr3, blr)


def _pad_rows(a, rows):
    return jnp.pad(a, ((0, rows - a.shape[0]), (0, 0)))


def kernel(x, positions, ev_norm, ev_w_in, sgu_gain, sgu_w_s, sgu_b, ev_w_out, od_norm, od_w_in,
           gla_w_lr, gla_b_lr, od_w_out, final_norm):
    b, s, d = x.shape
    t = b * s
    x2 = x.reshape(t, d)

    h0 = _rmsnorm(x2, ev_norm[0])
    proj0 = _inproj(h0, ev_w_in[0].astype(BF16)).reshape(b, s, -1)
    dk_ret = proj0.shape[2] // 12 // RET_HEADS
    half = dk_ret // 2
    inv = jnp.power(ROPE_BASE, -jnp.arange(half, dtype=F32) / half).reshape(1, half)
    posf = positions.astype(F32).reshape(b, s, 1)
    ret = _retention(proj0, posf, inv)
    sgu = _sgu(proj0, sgu_gain[0], sgu_w_s[0], sgu_b[0])
    w_out0 = ev_w_out[0].astype(BF16)
    kr = ret.shape[2]
    x1, h1 = _outproj_mid(ret.reshape(t, kr), sgu.reshape(t, -1), x2, w_out0[:kr], w_out0[kr:], od_norm[0])

    w_in1 = od_w_in[0]
    n_main = w_in1.shape[1] - GLA_RANK
    w_lr_in = jnp.pad(jnp.tile(w_in1[:, n_main:], (1, 3)), ((0, 0), (0, LANES - 3 * GLA_RANK)))
    proj1, lr = _inproj(h1, w_in1[:, :n_main].astype(BF16), w_lr_in.astype(BF16))
    w_hi, w_lo = _split_bf16(gla_w_lr[0])
    wlr3 = _pad_rows(jnp.concatenate([w_hi, w_hi, w_lo], axis=0), LANES)
    mixed = _gla(proj1.reshape(b, s, -1), lr.reshape(b, s, LANES), wlr3, gla_b_lr[0].reshape(1, -1))
    out = _outproj_final(mixed.reshape(t, -1), x1, od_w_out[0].astype(BF16), final_norm)
    return out.reshape(b, s, d)
```

```python
import functools

import numpy as np
import jax
import jax.numpy as jnp
from jax import lax
from jax.experimental import pallas as pl
from jax.experimental.pallas import tpu as pltpu

F32 = jnp.float32
BF16 = jnp.bfloat16

EPS = 1e-6
RET_HEADS = 4
RET_CHUNK = 128
ROPE_BASE = 10000.0
SGU_GROUPS = 8
SGU_CHUNK = 128
GLA_HEADS = 8
GLA_RANK = 16
GLA_TAU = 16.0
GLA_BLOCK = 128
GLA_SAFE_LOG_DECAY = 60.0

LANES = 128
VMEM_LIMIT = 56 * 1024 * 1024

NORM_TM = 512
PROJ_TM = 1024
PROJ_TN = 1024
OUT_TM = 512
RET_TS = 512
SGU_TS = 512
GLA_TS = 512


def _params(*sem):
    return pltpu.CompilerParams(dimension_semantics=sem, vmem_limit_bytes=VMEM_LIMIT)


def _silu(g):
    return g * (1.0 / (1.0 + jnp.exp(-g)))


def _rms(x, gain):
    return x * lax.rsqrt(jnp.mean(x * x, axis=-1, keepdims=True) + EPS) * gain


def _rmsnorm_kernel(x_ref, g_ref, o_ref):
    o_ref[...] = _rms(x_ref[...], g_ref[...]).astype(o_ref.dtype)


def _rmsnorm(x, gain):
    t, d = x.shape
    tm = min(NORM_TM, t)
    return pl.pallas_call(
        _rmsnorm_kernel,
        out_shape=jax.ShapeDtypeStruct((t, d), BF16),
        grid=(t // tm,),
        in_specs=[pl.BlockSpec((tm, d), lambda i: (i, 0)),
                  pl.BlockSpec((1, d), lambda i: (0, 0))],
        out_specs=pl.BlockSpec((tm, d), lambda i: (i, 0)),
        compiler_params=_params("parallel"),
        name="rmsnorm",
    )(x, gain.reshape(1, d))


def _inproj_kernel(h_ref, w_ref, o_ref):
    o_ref[...] = jnp.dot(h_ref[...], w_ref[...], preferred_element_type=F32).astype(o_ref.dtype)


def _inproj_lr_kernel(h_ref, w_ref, wlr_ref, o_ref, lr_ref):
    o_ref[...] = jnp.dot(h_ref[...], w_ref[...], preferred_element_type=F32).astype(o_ref.dtype)

    @pl.when(pl.program_id(1) == 0)
    def _():
        lr_ref[...] = jnp.dot(h_ref[...], wlr_ref[...], preferred_element_type=F32)


def _inproj(h, w, wlr=None):
    t, d = h.shape
    n = w.shape[1]
    tm, tn = min(PROJ_TM, t), min(PROJ_TN, n)
    grid = (t // tm, n // tn)
    h_spec = pl.BlockSpec((tm, d), lambda i, j: (i, 0))
    w_spec = pl.BlockSpec((d, tn), lambda i, j: (0, j))
    o_spec = pl.BlockSpec((tm, tn), lambda i, j: (i, j))
    if wlr is None:
        return pl.pallas_call(
            _inproj_kernel,
            out_shape=jax.ShapeDtypeStruct((t, n), BF16),
            grid=grid, in_specs=[h_spec, w_spec], out_specs=o_spec,
            compiler_params=_params("parallel", "arbitrary"),
            name="inproj",
        )(h, w)
    return pl.pallas_call(
        _inproj_lr_kernel,
        out_shape=(jax.ShapeDtypeStruct((t, n), BF16), jax.ShapeDtypeStruct((t, LANES), F32)),
        grid=grid,
        in_specs=[h_spec, w_spec, pl.BlockSpec((d, LANES), lambda i, j: (0, 0))],
        out_specs=(o_spec, pl.BlockSpec((tm, LANES), lambda i, j: (i, 0))),
        compiler_params=_params("parallel", "arbitrary"),
        name="inproj_lr",
    )(h, w, wlr)


def _retention_tables(dk):
    c = RET_CHUNK
    h = np.arange(RET_HEADS, dtype=np.float64)
    log_gamma = np.log1p(-np.power(2.0, -5.0 - h))
    idx = np.arange(c, dtype=np.float64)
    rel = idx[:, None] - idx[None, :]
    dmat = np.where(rel >= 0, np.exp(log_gamma[:, None, None] * np.maximum(rel, 0.0)), 0.0)
    qdec = np.exp(log_gamma[:, None] * (idx + 1.0))
    kdec = np.exp(log_gamma[:, None] * (c - 1.0 - idx))
    cdec = np.exp(log_gamma * c)
    lanes = lambda a: np.broadcast_to(a[:, :, None], (RET_HEADS, c, LANES))
    return (jnp.asarray(dmat, F32), jnp.asarray(lanes(qdec), F32), jnp.asarray(lanes(kdec), F32),
            [float(v) for v in cdec], float(dk) ** -0.5)


def _retention_kernel(q_ref, k_ref, v_ref, g_ref, pos_ref, inv_ref, dmat_ref, qdec_ref, kdec_ref,
                      o_ref, state_ref, *, cdec, scale, dk, dv):
    c = RET_CHUNK
    half = dk // 2
    ts = q_ref.shape[1]

    @pl.when(pl.program_id(1) == 0)
    def _():
        state_ref[...] = jnp.zeros_like(state_ref)

    def chunk(ci, carry):
        rows = pl.ds(pl.multiple_of(ci * c, c), c)
        ang = pos_ref[0, rows, :] * inv_ref[...]
        cos, sin = jnp.cos(ang), jnp.sin(ang)
        cos_q, sin_q = cos * scale, sin * scale
        for h in range(RET_HEADS):
            q = q_ref[0, rows, h * dk:(h + 1) * dk].astype(F32)
            k = k_ref[0, rows, h * dk:(h + 1) * dk].astype(F32)
            q1, q2 = q[:, :half], q[:, half:]
            k1, k2 = k[:, :half], k[:, half:]
            qr1, qr2 = q1 * cos_q - q2 * sin_q, q1 * sin_q + q2 * cos_q
            kr1, kr2 = k1 * cos - k2 * sin, k1 * sin + k2 * cos
            qdec, kdec = qdec_ref[h], kdec_ref[h]
            qr = jnp.concatenate([qr1, qr2], axis=-1).astype(BF16)
            kr = jnp.concatenate([kr1, kr2], axis=-1).astype(BF16)
            qd = jnp.concatenate([qr1 * qdec, qr2 * qdec], axis=-1).astype(BF16)
            kd = jnp.concatenate([kr1 * kdec, kr2 * kdec], axis=-1).astype(BF16)
            s = lax.dot_general(qr, kr, (((1,), (1,)), ((), ())), preferred_element_type=F32)
            p = (s * dmat_ref[h]).astype(BF16)
            v = v_ref[0, rows, h * dv:(h + 1) * dv]
            st = state_ref[h]
            o = (jnp.dot(p, v, preferred_element_type=F32)
                 + jnp.dot(qd, st.astype(BF16), preferred_element_type=F32))
            state_ref[h] = cdec[h] * st + lax.dot_general(
                kd, v, (((0,), (0,)), ((), ())), preferred_element_type=F32)
            on = o * lax.rsqrt(jnp.mean(o * o, axis=-1, keepdims=True) + EPS)
            g = g_ref[0, rows, h * dv:(h + 1) * dv].astype(F32)
            o_ref[0, rows, h * dv:(h + 1) * dv] = (on * _silu(g)).astype(o_ref.dtype)
        return carry

    lax.fori_loop(0, ts // c, chunk, 0)


def _retention(proj, posf, inv):
    b, s, _ = proj.shape
    qk_w = proj.shape[2] // 12
    v_w = 2 * qk_w
    dk, dv = qk_w // RET_HEADS, v_w // RET_HEADS
    ts = min(RET_TS, s)
    dmat, qdec, kdec, cdec, scale = _retention_tables(dk)
    const3 = lambda shape: pl.BlockSpec(shape, lambda bi, i: (0, 0, 0))
    return pl.pallas_call(
        functools.partial(_retention_kernel, cdec=cdec, scale=scale, dk=dk, dv=dv),
        out_shape=jax.ShapeDtypeStruct((b, s, v_w), BF16),
        grid=(b, s // ts),
        in_specs=[pl.BlockSpec((1, ts, qk_w), lambda bi, i: (bi, i, 0)),
                  pl.BlockSpec((1, ts, qk_w), lambda bi, i: (bi, i, 1)),
                  pl.BlockSpec((1, ts, v_w), lambda bi, i: (bi, i, 1)),
                  pl.BlockSpec((1, ts, v_w), lambda bi, i: (bi, i, 2)),
                  pl.BlockSpec((1, ts, 1), lambda bi, i: (bi, i, 0)),
                  pl.BlockSpec((1, dk // 2), lambda bi, i: (0, 0)),
                  const3(dmat.shape), const3(qdec.shape), const3(kdec.shape)],
        out_specs=pl.BlockSpec((1, ts, v_w), lambda bi, i: (bi, i, 0)),
        scratch_shapes=[pltpu.VMEM((RET_HEADS, dk, dv), F32)],
        compiler_params=_params("parallel", "arbitrary"),
        name="retention",
    )(proj, proj, proj, proj, posf, inv, dmat, qdec, kdec)


def _sgu_kernel(u_ref, v_ref, g_ref, gain_ref, w_ref, b_ref, o_ref, *, dg):
    c = SGU_CHUNK
    ts = u_ref.shape[1]
    row = lax.broadcasted_iota(jnp.int32, (c, c), 0)
    col = lax.broadcasted_iota(jnp.int32, (c, c), 1)
    causal = row >= col
    wm = [jnp.where(causal, w_ref[g], 0.0).astype(BF16) for g in range(SGU_GROUPS)]

    def chunk(ci, carry):
        rows = pl.ds(pl.multiple_of(ci * c, c), c)
        for g in range(SGU_GROUPS):
            cols = slice(g * dg, (g + 1) * dg)
            v = v_ref[0, rows, cols].astype(F32)
            d = v - jnp.mean(v, axis=-1, keepdims=True)
            var = jnp.mean(d * d, axis=-1, keepdims=True)
            vn = d * lax.rsqrt(var + EPS) * gain_ref[:, cols]
            sg = jnp.dot(wm[g], vn.astype(BF16), preferred_element_type=F32) + b_ref[g]
            u = u_ref[0, rows, cols].astype(F32)
            gate = g_ref[0, rows, cols].astype(F32)
            o_ref[0, rows, cols] = (u * sg * _silu(gate)).astype(o_ref.dtype)
        return carry

    lax.fori_loop(0, ts // c, chunk, 0)


def _sgu(proj, gain, w_s, b_s):
    b, s, _ = proj.shape
    w = proj.shape[2] // 6
    dg = w // SGU_GROUPS
    ts = min(SGU_TS, s)
    bias = jnp.broadcast_to(b_s[:, :, None], (SGU_GROUPS, SGU_CHUNK, dg))
    const3 = lambda shape: pl.BlockSpec(shape, lambda bi, i: (0, 0, 0))
    return pl.pallas_call(
        functools.partial(_sgu_kernel, dg=dg),
        out_shape=jax.ShapeDtypeStruct((b, s, w), BF16),
        grid=(b, s // ts),
        in_specs=[pl.BlockSpec((1, ts, w), lambda bi, i: (bi, i, 3)),
                  pl.BlockSpec((1, ts, w), lambda bi, i: (bi, i, 4)),
                  pl.BlockSpec((1, ts, w), lambda bi, i: (bi, i, 5)),
                  pl.BlockSpec((1, w), lambda bi, i: (0, 0)),
                  const3(w_s.shape), const3(bias.shape)],
        out_specs=pl.BlockSpec((1, ts, w), lambda bi, i: (bi, i, 0)),
        compiler_params=_params("parallel", "parallel"),
        name="sgu",
    )(proj, proj, proj, gain.reshape(1, w), w_s, bias)


def _outproj_mid_kernel(a_ref, b_ref, x_ref, wa_ref, wb_ref, gn_ref, x1_ref, h1_ref):
    acc = (jnp.dot(a_ref[...], wa_ref[...], preferred_element_type=F32)
           + jnp.dot(b_ref[...], wb_ref[...], preferred_element_type=F32))
    x1 = x_ref[...] + acc
    x1_ref[...] = x1
    h1_ref[...] = _rms(x1, gn_ref[...]).astype(h1_ref.dtype)


def _outproj_final_kernel(a_ref, x_ref, w_ref, gn_ref, o_ref):
    x1 = x_ref[...] + jnp.dot(a_ref[...], w_ref[...], preferred_element_type=F32)
    o_ref[...] = _rms(x1, gn_ref[...])


def _resident(shape):
    return pl.BlockSpec(shape, lambda i: (0, 0), pipeline_mode=pl.Buffered(1))


def _outproj_mid(a, b, x, wa, wb, gain):
    t, d = x.shape
    ka, kb = a.shape[1], b.shape[1]
    tm = min(OUT_TM, t)
    row = lambda w: pl.BlockSpec((tm, w), lambda i: (i, 0))
    return pl.pallas_call(
        _outproj_mid_kernel,
        out_shape=(jax.ShapeDtypeStruct((t, d), F32), jax.ShapeDtypeStruct((t, d), BF16)),
        grid=(t // tm,),
        in_specs=[row(ka), row(kb), row(d), _resident((ka, d)), _resident((kb, d)), _resident((1, d))],
        out_specs=(row(d), row(d)),
        compiler_params=_params("parallel"),
        name="outproj_mid",
    )(a, b, x, wa, wb, gain.reshape(1, d))


def _outproj_final(a, x, w, gain):
    t, d = x.shape
    ka = a.shape[1]
    tm = min(OUT_TM, t)
    row = lambda w_: pl.BlockSpec((tm, w_), lambda i: (i, 0))
    return pl.pallas_call(
        _outproj_final_kernel,
        out_shape=jax.ShapeDtypeStruct((t, d), F32),
        grid=(t // tm,),
        in_specs=[row(ka), row(d), _resident((ka, d)), _resident((1, d))],
        out_specs=row(d),
        compiler_params=_params("parallel"),
        name="outproj_final",
    )(a, x, w, gain.reshape(1, d))


def _split_bf16(x):
    hi = x.astype(BF16)
    lo = (x - hi.astype(F32)).astype(BF16)
    return hi, lo


def _gla_kernel(q_ref, k_ref, v_ref, g_ref, lr_ref, wlr_ref, blr_ref, o_ref,
                state_ref, state0_ref, bc_ref, kf_ref, *, scale):
    c = GLA_BLOCK
    ts = q_ref.shape[1]
    n_blocks = ts // c

    @pl.when(pl.program_id(2) == 0)
    def _():
        state_ref[...] = jnp.zeros_like(state_ref)

    state0_ref[...] = state_ref[...]

    lr_hi, lr_lo = _split_bf16(lr_ref[0])
    lane = lax.broadcasted_iota(jnp.int32, lr_hi.shape, 1)
    lr3 = jnp.where((lane >= GLA_RANK) & (lane < 2 * GLA_RANK), lr_lo, lr_hi)
    z = jnp.dot(lr3, wlr_ref[...], preferred_element_type=F32) + blr_ref[...]
    log_alpha = (jnp.minimum(z, 0.0) - jnp.log(1.0 + jnp.exp(-jnp.abs(z)))) * (1.0 / GLA_TAU)

    row = lax.broadcasted_iota(jnp.int32, (c, c), 0)
    col = lax.broadcasted_iota(jnp.int32, (c, c), 1)
    causal = row >= col
    tri = jnp.where(causal, 1.0, 0.0).astype(BF16)
    tri2 = jnp.concatenate([tri, tri], axis=1)

    worst = None
    for ci in range(n_blocks):
        rows = slice(ci * c, (ci + 1) * c)
        la_hi, la_lo = _split_bf16(log_alpha[rows])
        bc = jnp.dot(tri2, jnp.concatenate([la_hi, la_lo], axis=0), preferred_element_type=F32)
        bc_ref[rows, :] = bc
        worst = bc[c - 1:c, :] if worst is None else jnp.minimum(worst, bc[c - 1:c, :])

    def factored_scores(ci, bc, q, k, q_dec, k_dec):
        mid = bc[c // 2 - 1:c // 2, :]
        last = bc[c - 1:c, :]
        q_t = (q_dec * jnp.exp(-mid)).astype(BF16)
        k_t = (k_dec * jnp.exp(mid - last)).astype(BF16)
        s = lax.dot_general(q_t, k_t, (((1,), (1,)), ((), ())), preferred_element_type=F32)
        return jnp.where(causal, s, 0.0).astype(BF16)

    def exact_scores(ci, bc, q, k, q_dec, k_dec):
        kf_ref[...] = k

        def column(s, p):
            w = jnp.exp(jnp.minimum(bc - bc_ref[pl.ds(ci * c + s, 1), :], 0.0))
            val = jnp.sum(q * w * kf_ref[pl.ds(s, 1), :], axis=-1, keepdims=True)
            return jnp.where(col == s, val, p)

        p = lax.fori_loop(0, c, column, jnp.zeros((c, c), F32))
        return jnp.where(causal, p, 0.0).astype(BF16)

    def run_blocks(scores_fn):
        for ci in range(n_blocks):
            rows = slice(ci * c, (ci + 1) * c)
            bc = bc_ref[rows, :]
            last = bc[c - 1:c, :]
            q = q_ref[0, rows, :].astype(F32)
            k = k_ref[0, rows, :].astype(F32)
            v = v_ref[0, rows, :]
            q_dec = q * jnp.exp(bc)
            k_dec = k * jnp.exp(last - bc)
            q_in, k_out = q_dec.astype(BF16), k_dec.astype(BF16)
            p = scores_fn(ci, bc, q, k, q_dec, k_dec)
            st = state_ref[...]
            o = (jnp.dot(p, v, preferred_element_type=F32)
                 + lax.dot_general(q_in, st.astype(BF16), (((1,), (1,)), ((), ())),
                                   preferred_element_type=F32))
            state_ref[...] = st * jnp.exp(last) + lax.dot_general(
                v, k_out, (((0,), (0,)), ((), ())), preferred_element_type=F32)
            on = o * lax.rsqrt(jnp.mean(o * o, axis=-1, keepdims=True) + EPS / (scale * scale))
            g = g_ref[0, rows, :].astype(F32)
            o_ref[0, rows, :] = (on * _silu(g)).astype(o_ref.dtype)

    run_blocks(factored_scores)

    @pl.when(jnp.logical_not(jnp.min(worst) > -GLA_SAFE_LOG_DECAY))
    def _():
        state_ref[...] = state0_ref[...]
        run_blocks(exact_scores)


def _gla(proj, lr, wlr3, blr):
    b, s, _ = proj.shape
    qk_w = proj.shape[2] // 6
    v_w = 2 * qk_w
    dk, dv = qk_w // GLA_HEADS, v_w // GLA_HEADS
    ts = min(GLA_TS, s)
    nh = GLA_HEADS
    return pl.pallas_call(
        functools.partial(_gla_kernel, scale=float(dk) ** -0.5),
        out_shape=jax.ShapeDtypeStruct((b, s, v_w), BF16),
        grid=(b, nh, s // ts),
        in_specs=[pl.BlockSpec((1, ts, dk), lambda bi, h, i: (bi, i, h)),
                  pl.BlockSpec((1, ts, dk), lambda bi, h, i: (bi, i, nh + h)),
                  pl.BlockSpec((1, ts, dv), lambda bi, h, i: (bi, i, nh + h)),
                  pl.BlockSpec((1, ts, dv), lambda bi, h, i: (bi, i, 2 * nh + h)),
                  pl.BlockSpec((1, ts, LANES), lambda bi, h, i: (bi, i, 0)),
                  pl.BlockSpec((LANES, dk), lambda bi, h, i: (0, h)),
                  pl.BlockSpec((1, dk), lambda bi, h, i: (0, h))],
        out_specs=pl.BlockSpec((1, ts, dv), lambda bi, h, i: (bi, i, h)),
        scratch_shapes=[pltpu.VMEM((dv, dk), F32),
                        pltpu.VMEM((dv, dk), F32),
                        pltpu.VMEM((ts, dk), F32),
                        pltpu.VMEM((GLA_BLOCK, dk), F32)],
        compiler_params=_params("parallel", "parallel", "arbitrary"),
        name="gla",
    )(proj, proj, proj, proj, lr, wlr3, blr)


def _pad_rows(a, rows):
    return jnp.pad(a, ((0, rows - a.shape[0]), (0, 0)))


def kernel(x, positions, ev_norm, ev_w_in, sgu_gain, sgu_w_s, sgu_b, ev_w_out, od_norm, od_w_in,
           gla_w_lr, gla_b_lr, od_w_out, final_norm):
    b, s, d = x.shape
    t = b * s
    x2 = x.reshape(t, d)

    h0 = _rmsnorm(x2, ev_norm[0])
    proj0 = _inproj(h0, ev_w_in[0].astype(BF16)).reshape(b, s, -1)
    dk_ret = proj0.shape[2] // 12 // RET_HEADS
    half = dk_ret // 2
    inv = jnp.power(ROPE_BASE, -jnp.arange(half, dtype=F32) / half).reshape(1, half)
    posf = positions.astype(F32).reshape(b, s, 1)
    ret = _retention(proj0, posf, inv)
    sgu = _sgu(proj0, sgu_gain[0], sgu_w_s[0], sgu_b[0])
    w_out0 = ev_w_out[0].astype(BF16)
    kr = ret.shape[2]
    x1, h1 = _outproj_mid(ret.reshape(t, kr), sgu.reshape(t, -1), x2, w_out0[:kr], w_out0[kr:], od_norm[0])

    w_in1 = od_w_in[0]
    n_main = w_in1.shape[1] - GLA_RANK
    w_lr_in = jnp.pad(jnp.tile(w_in1[:, n_main:], (1, 3)), ((0, 0), (0, LANES - 3 * GLA_RANK)))
    proj1, lr = _inproj(h1, w_in1[:, :n_main].astype(BF16), w_lr_in.astype(BF16))
    w_hi, w_lo = _split_bf16(gla_w_lr[0])
    wlr3 = _pad_rows(jnp.concatenate([w_hi, w_hi, w_lo], axis=0), LANES)
    mixed = _gla(proj1.reshape(b, s, -1), lr.reshape(b, s, LANES), wlr3, gla_b_lr[0].reshape(1, -1))
    out = _outproj_final(mixed.reshape(t, -1), x1, od_w_out[0].astype(BF16), final_norm)
    return out.reshape(b, s, d)
```

```python
import functools

import numpy as np
import jax
import jax.numpy as jnp
from jax import lax
from jax.experimental import pallas as pl
from jax.experimental.pallas import tpu as pltpu

F32 = jnp.float32
BF16 = jnp.bfloat16

EPS = 1e-6
RET_HEADS = 4
RET_BLOCK = 256
ROPE_BASE = 10000.0
SGU_GROUPS = 8
SGU_CHUNK = 128
GLA_HEADS = 8
GLA_RANK = 16
GLA_TAU = 16.0
GLA_BLOCK = 128
GLA_SAFE_LOG_DECAY = 60.0

LANES = 128
VMEM_LIMIT = 56 * 1024 * 1024

NORM_TM = 512
PROJ_TM = 1024
PROJ_TN = 1024
OUT_TM = 512
RET_TS = 512
SGU_TS = 512
GLA_TS = 512


def _params(*sem):
    return pltpu.CompilerParams(dimension_semantics=sem, vmem_limit_bytes=VMEM_LIMIT)


def _silu(g):
    return g * (1.0 / (1.0 + jnp.exp(-g)))


def _rms(x, gain):
    return x * lax.rsqrt(jnp.mean(x * x, axis=-1, keepdims=True) + EPS) * gain


def _rmsnorm_kernel(x_ref, g_ref, o_ref):
    o_ref[...] = _rms(x_ref[...], g_ref[...]).astype(o_ref.dtype)


def _rmsnorm(x, gain):
    t, d = x.shape
    tm = min(NORM_TM, t)
    return pl.pallas_call(
        _rmsnorm_kernel,
        out_shape=jax.ShapeDtypeStruct((t, d), BF16),
        grid=(t // tm,),
        in_specs=[pl.BlockSpec((tm, d), lambda i: (i, 0)),
                  pl.BlockSpec((1, d), lambda i: (0, 0))],
        out_specs=pl.BlockSpec((tm, d), lambda i: (i, 0)),
        compiler_params=_params("parallel"),
        name="rmsnorm",
    )(x, gain.reshape(1, d))


def _inproj_kernel(h_ref, w_ref, o_ref, wbf_ref):
    @pl.when(pl.program_id(1) == 0)
    def _():
        wbf_ref[...] = w_ref[...].astype(BF16)

    o_ref[...] = jnp.dot(h_ref[...], wbf_ref[...], preferred_element_type=F32).astype(o_ref.dtype)


def _inproj(h, w, n):
    t, d = h.shape
    tm, tn = min(PROJ_TM, t), min(PROJ_TN, n)
    return pl.pallas_call(
        _inproj_kernel,
        out_shape=jax.ShapeDtypeStruct((t, n), BF16),
        grid=(n // tn, t // tm),
        in_specs=[pl.BlockSpec((tm, d), lambda j, i: (i, 0)),
                  pl.BlockSpec((d, tn), lambda j, i: (0, j))],
        out_specs=pl.BlockSpec((tm, tn), lambda j, i: (i, j)),
        scratch_shapes=[pltpu.VMEM((d, tn), BF16)],
        compiler_params=_params("parallel", "arbitrary"),
        name="inproj",
    )(h, w)


def _retention_tables(dk):
    c = RET_BLOCK
    h = np.arange(RET_HEADS, dtype=np.float64)
    log_gamma = np.log1p(-np.power(2.0, -5.0 - h))
    idx = np.arange(c, dtype=np.float64)
    rel = idx[:, None] - idx[None, :]
    dmat = np.where(rel >= 0, np.exp(log_gamma[:, None, None] * np.maximum(rel, 0.0)), 0.0)
    qdec = np.exp(log_gamma[:, None] * (idx + 1.0))
    kdec = np.exp(log_gamma[:, None] * (c - 1.0 - idx))
    cdec = np.exp(log_gamma * c)
    lanes = lambda a: np.broadcast_to(a[:, :, None], (RET_HEADS, c, LANES))
    return (jnp.asarray(dmat, F32), jnp.asarray(lanes(qdec), F32), jnp.asarray(lanes(kdec), F32),
            [float(v) for v in cdec], float(dk) ** -0.5)


def _retention_kernel(q_ref, k_ref, v_ref, g_ref, pos_ref, inv_ref, dmat_ref, qdec_ref, kdec_ref,
                      o_ref, state_ref, *, cdec, scale, dk, dv):
    c = RET_BLOCK
    half = dk // 2
    ts = q_ref.shape[1]

    @pl.when(pl.program_id(1) == 0)
    def _():
        state_ref[...] = jnp.zeros_like(state_ref)

    def chunk(ci, carry):
        rows = pl.ds(pl.multiple_of(ci * c, c), c)
        ang = pos_ref[0, rows, :] * inv_ref[...]
        cos, sin = jnp.cos(ang), jnp.sin(ang)
        cos_q, sin_q = cos * scale, sin * scale
        for h in range(RET_HEADS):
            q = q_ref[0, rows, h * dk:(h + 1) * dk].astype(F32)
            k = k_ref[0, rows, h * dk:(h + 1) * dk].astype(F32)
            q1, q2 = q[:, :half], q[:, half:]
            k1, k2 = k[:, :half], k[:, half:]
            qr1, qr2 = q1 * cos_q - q2 * sin_q, q1 * sin_q + q2 * cos_q
            kr1, kr2 = k1 * cos - k2 * sin, k1 * sin + k2 * cos
            qdec, kdec = qdec_ref[h], kdec_ref[h]
            qr = jnp.concatenate([qr1, qr2], axis=-1).astype(BF16)
            kr = jnp.concatenate([kr1, kr2], axis=-1).astype(BF16)
            qd = jnp.concatenate([qr1 * qdec, qr2 * qdec], axis=-1).astype(BF16)
            kd = jnp.concatenate([kr1 * kdec, kr2 * kdec], axis=-1).astype(BF16)
            s = lax.dot_general(qr, kr, (((1,), (1,)), ((), ())), preferred_element_type=F32)
            p = (s * dmat_ref[h]).astype(BF16)
            v = v_ref[0, rows, h * dv:(h + 1) * dv]
            st = state_ref[h]
            o = (jnp.dot(p, v, preferred_element_type=F32)
                 + jnp.dot(qd, st.astype(BF16), preferred_element_type=F32))
            state_ref[h] = cdec[h] * st + lax.dot_general(
                kd, v, (((0,), (0,)), ((), ())), preferred_element_type=F32)
            on = o * lax.rsqrt(jnp.mean(o * o, axis=-1, keepdims=True) + EPS)
            g = g_ref[0, rows, h * dv:(h + 1) * dv].astype(F32)
            o_ref[0, rows, h * dv:(h + 1) * dv] = (on * _silu(g)).astype(o_ref.dtype)
        return carry

    lax.fori_loop(0, ts // c, chunk, 0)


def _retention(proj, posf, inv):
    b, s, _ = proj.shape
    qk_w = proj.shape[2] // 12
    v_w = 2 * qk_w
    dk, dv = qk_w // RET_HEADS, v_w // RET_HEADS
    ts = min(RET_TS, s)
    dmat, qdec, kdec, cdec, scale = _retention_tables(dk)
    const3 = lambda shape: pl.BlockSpec(shape, lambda bi, i: (0, 0, 0))
    return pl.pallas_call(
        functools.partial(_retention_kernel, cdec=cdec, scale=scale, dk=dk, dv=dv),
        out_shape=jax.ShapeDtypeStruct((b, s, v_w), BF16),
        grid=(b, s // ts),
        in_specs=[pl.BlockSpec((1, ts, qk_w), lambda bi, i: (bi, i, 0)),
                  pl.BlockSpec((1, ts, qk_w), lambda bi, i: (bi, i, 1)),
                  pl.BlockSpec((1, ts, v_w), lambda bi, i: (bi, i, 1)),
                  pl.BlockSpec((1, ts, v_w), lambda bi, i: (bi, i, 2)),
                  pl.BlockSpec((1, ts, 1), lambda bi, i: (bi, i, 0)),
                  pl.BlockSpec((1, dk // 2), lambda bi, i: (0, 0)),
                  const3(dmat.shape), const3(qdec.shape), const3(kdec.shape)],
        out_specs=pl.BlockSpec((1, ts, v_w), lambda bi, i: (bi, i, 0)),
        scratch_shapes=[pltpu.VMEM((RET_HEADS, dk, dv), F32)],
        compiler_params=_params("parallel", "arbitrary"),
        name="retention",
    )(proj, proj, proj, proj, posf, inv, dmat, qdec, kdec)


def _sgu_kernel(u_ref, v_ref, g_ref, gain_ref, w_ref, b_ref, o_ref, *, dg):
    c = SGU_CHUNK
    ts = u_ref.shape[1]
    row = lax.broadcasted_iota(jnp.int32, (c, c), 0)
    col = lax.broadcasted_iota(jnp.int32, (c, c), 1)
    causal = row >= col
    wm = [jnp.where(causal, w_ref[g], 0.0).astype(BF16) for g in range(SGU_GROUPS)]

    def chunk(ci, carry):
        rows = pl.ds(pl.multiple_of(ci * c, c), c)
        for g in range(SGU_GROUPS):
            cols = slice(g * dg, (g + 1) * dg)
            v = v_ref[0, rows, cols].astype(F32)
            d = v - jnp.mean(v, axis=-1, keepdims=True)
            var = jnp.mean(d * d, axis=-1, keepdims=True)
            vn = d * lax.rsqrt(var + EPS) * gain_ref[:, cols]
            sg = jnp.dot(wm[g], vn.astype(BF16), preferred_element_type=F32) + b_ref[g]
            u = u_ref[0, rows, cols].astype(F32)
            gate = g_ref[0, rows, cols].astype(F32)
            o_ref[0, rows, cols] = (u * sg * _silu(gate)).astype(o_ref.dtype)
        return carry

    lax.fori_loop(0, ts // c, chunk, 0)


def _sgu(proj, gain, w_s, b_s):
    b, s, _ = proj.shape
    w = proj.shape[2] // 6
    dg = w // SGU_GROUPS
    ts = min(SGU_TS, s)
    bias = jnp.broadcast_to(b_s[:, :, None], (SGU_GROUPS, SGU_CHUNK, dg))
    const3 = lambda shape: pl.BlockSpec(shape, lambda bi, i: (0, 0, 0))
    return pl.pallas_call(
        functools.partial(_sgu_kernel, dg=dg),
        out_shape=jax.ShapeDtypeStruct((b, s, w), BF16),
        grid=(b, s // ts),
        in_specs=[pl.BlockSpec((1, ts, w), lambda bi, i: (bi, i, 3)),
                  pl.BlockSpec((1, ts, w), lambda bi, i: (bi, i, 4)),
                  pl.BlockSpec((1, ts, w), lambda bi, i: (bi, i, 5)),
                  pl.BlockSpec((1, w), lambda bi, i: (0, 0)),
                  const3(w_s.shape), const3(bias.shape)],
        out_specs=pl.BlockSpec((1, ts, w), lambda bi, i: (bi, i, 0)),
        compiler_params=_params("parallel", "parallel"),
        name="sgu",
    )(proj, proj, proj, gain.reshape(1, w), w_s, bias)


def _outproj_mid_kernel(a_ref, b_ref, x_ref, wa_ref, wb_ref, gn_ref, wlr_ref, x1_ref, h1_ref, lr_ref):
    acc = (jnp.dot(a_ref[...], wa_ref[...], preferred_element_type=F32)
           + jnp.dot(b_ref[...], wb_ref[...], preferred_element_type=F32))
    x1 = x_ref[...] + acc
    x1_ref[...] = x1
    h1 = _rms(x1, gn_ref[...]).astype(h1_ref.dtype)
    h1_ref[...] = h1
    lr_ref[...] = jnp.dot(h1, wlr_ref[...], preferred_element_type=F32)


def _outproj_final_kernel(a_ref, x_ref, w_ref, gn_ref, o_ref):
    x1 = x_ref[...] + jnp.dot(a_ref[...], w_ref[...], preferred_element_type=F32)
    o_ref[...] = _rms(x1, gn_ref[...])


def _resident(shape):
    return pl.BlockSpec(shape, lambda i: (0, 0), pipeline_mode=pl.Buffered(1))


def _outproj_mid(a, b, x, wa, wb, gain, wlr):
    t, d = x.shape
    ka, kb = a.shape[1], b.shape[1]
    tm = min(OUT_TM, t)
    row = lambda w: pl.BlockSpec((tm, w), lambda i: (i, 0))
    return pl.pallas_call(
        _outproj_mid_kernel,
        out_shape=(jax.ShapeDtypeStruct((t, d), F32), jax.ShapeDtypeStruct((t, d), BF16),
                   jax.ShapeDtypeStruct((t, LANES), F32)),
        grid=(t // tm,),
        in_specs=[row(ka), row(kb), row(d), _resident((ka, d)), _resident((kb, d)), _resident((1, d)),
                  _resident((d, LANES))],
        out_specs=(row(d), row(d), row(LANES)),
        compiler_params=_params("parallel"),
        name="outproj_mid",
    )(a, b, x, wa, wb, gain.reshape(1, d), wlr)


def _outproj_final(a, x, w, gain):
    t, d = x.shape
    ka = a.shape[1]
    tm = min(OUT_TM, t)
    row = lambda w_: pl.BlockSpec((tm, w_), lambda i: (i, 0))
    return pl.pallas_call(
        _outproj_final_kernel,
        out_shape=jax.ShapeDtypeStruct((t, d), F32),
        grid=(t // tm,),
        in_specs=[row(ka), row(d), _resident((ka, d)), _resident((1, d))],
        out_specs=row(d),
        compiler_params=_params("parallel"),
        name="outproj_final",
    )(a, x, w, gain.reshape(1, d))


def _split_bf16(x):
    hi = x.astype(BF16)
    lo = (x - hi.astype(F32)).astype(BF16)
    return hi, lo


def _gla_kernel(q_ref, k_ref, v_ref, g_ref, lr_ref, wlr_ref, blr_ref, o_ref,
                state_ref, state0_ref, bc_ref, kf_ref, *, scale):
    c = GLA_BLOCK
    ts = q_ref.shape[1]
    n_blocks = ts // c

    @pl.when(pl.program_id(2) == 0)
    def _():
        state_ref[...] = jnp.zeros_like(state_ref)

    state0_ref[...] = state_ref[...]

    lr_hi, lr_lo = _split_bf16(lr_ref[0])
    lane = lax.broadcasted_iota(jnp.int32, lr_hi.shape, 1)
    lr3 = jnp.where((lane >= GLA_RANK) & (lane < 2 * GLA_RANK), lr_lo, lr_hi)
    z = jnp.dot(lr3, wlr_ref[...], preferred_element_type=F32) + blr_ref[...]
    log_alpha = (jnp.minimum(z, 0.0) - jnp.log(1.0 + jnp.exp(-jnp.abs(z)))) * (1.0 / GLA_TAU)

    row = lax.broadcasted_iota(jnp.int32, (c, c), 0)
    col = lax.broadcasted_iota(jnp.int32, (c, c), 1)
    causal = row >= col
    tri = jnp.where(causal, 1.0, 0.0).astype(BF16)
    tri2 = jnp.concatenate([tri, tri], axis=1)

    worst = None
    for ci in range(n_blocks):
        rows = slice(ci * c, (ci + 1) * c)
        la_hi, la_lo = _split_bf16(log_alpha[rows])
        bc = jnp.dot(tri2, jnp.concatenate([la_hi, la_lo], axis=0), preferred_element_type=F32)
        bc_ref[rows, :] = bc
        worst = bc[c - 1:c, :] if worst is None else jnp.minimum(worst, bc[c - 1:c, :])

    def factored_scores(ci, bc, q, k, q_dec, k_dec):
        mid = bc[c // 2 - 1:c // 2, :]
        last = bc[c - 1:c, :]
        q_t = (q_dec * jnp.exp(-mid)).astype(BF16)
        k_t = (k_dec * jnp.exp(mid - last)).astype(BF16)
        s = lax.dot_general(q_t, k_t, (((1,), (1,)), ((), ())), preferred_element_type=F32)
        return jnp.where(causal, s, 0.0).astype(BF16)

    def exact_scores(ci, bc, q, k, q_dec, k_dec):
        kf_ref[...] = k

        def column(s, p):
            w = jnp.exp(jnp.minimum(bc - bc_ref[pl.ds(ci * c + s, 1), :], 0.0))
            val = jnp.sum(q * w * kf_ref[pl.ds(s, 1), :], axis=-1, keepdims=True)
            return jnp.where(col == s, val, p)

        p = lax.fori_loop(0, c, column, jnp.zeros((c, c), F32))
        return jnp.where(causal, p, 0.0).astype(BF16)

    def run_blocks(scores_fn):
        for ci in range(n_blocks):
            rows = slice(ci * c, (ci + 1) * c)
            bc = bc_ref[rows, :]
            last = bc[c - 1:c, :]
            q = q_ref[0, rows, :].astype(F32)
            k = k_ref[0, rows, :].astype(F32)
            v = v_ref[0, rows, :]
            q_dec = q * jnp.exp(bc)
            k_dec = k * jnp.exp(last - bc)
            q_in, k_out = q_dec.astype(BF16), k_dec.astype(BF16)
            p = scores_fn(ci, bc, q, k, q_dec, k_dec)
            st = state_ref[...]
            o = (jnp.dot(p, v, preferred_element_type=F32)
                 + lax.dot_general(q_in, st.astype(BF16), (((1,), (1,)), ((), ())),
                                   preferred_element_type=F32))
            state_ref[...] = st * jnp.exp(last) + lax.dot_general(
                v, k_out, (((0,), (0,)), ((), ())), preferred_element_type=F32)
            on = o * lax.rsqrt(jnp.mean(o * o, axis=-1, keepdims=True) + EPS / (scale * scale))
            g = g_ref[0, rows, :].astype(F32)
            o_ref[0, rows, :] = (on * _silu(g)).astype(o_ref.dtype)

    run_blocks(factored_scores)

    @pl.when(jnp.logical_not(jnp.min(worst) > -GLA_SAFE_LOG_DECAY))
    def _():
        state_ref[...] = state0_ref[...]
        run_blocks(exact_scores)


def _gla(proj, lr, wlr3, blr):
    b, s, _ = proj.shape
    qk_w = proj.shape[2] // 6
    v_w = 2 * qk_w
    dk, dv = qk_w // GLA_HEADS, v_w // GLA_HEADS
    ts = min(GLA_TS, s)
    nh = GLA_HEADS
    return pl.pallas_call(
        functools.partial(_gla_kernel, scale=float(dk) ** -0.5),
        out_shape=jax.ShapeDtypeStruct((b, s, v_w), BF16),
        grid=(b, nh, s // ts),
        in_specs=[pl.BlockSpec((1, ts, dk), lambda bi, h, i: (bi, i, h)),
                  pl.BlockSpec((1, ts, dk), lambda bi, h, i: (bi, i, nh + h)),
                  pl.BlockSpec((1, ts, dv), lambda bi, h, i: (bi, i, nh + h)),
                  pl.BlockSpec((1, ts, dv), lambda bi, h, i: (bi, i, 2 * nh + h)),
                  pl.BlockSpec((1, ts, LANES), lambda bi, h, i: (bi, i, 0)),
                  pl.BlockSpec((LANES, dk), lambda bi, h, i: (0, h)),
                  pl.BlockSpec((1, dk), lambda bi, h, i: (0, h))],
        out_specs=pl.BlockSpec((1, ts, dv), lambda bi, h, i: (bi, i, h)),
        scratch_shapes=[pltpu.VMEM((dv, dk), F32),
                        pltpu.VMEM((dv, dk), F32),
                        pltpu.VMEM((ts, dk), F32),
                        pltpu.VMEM((GLA_BLOCK, dk), F32)],
        compiler_params=_params("parallel", "parallel", "arbitrary"),
        name="gla",
    )(proj, proj, proj, proj, lr, wlr3, blr)


def _pad_rows(a, rows):
    return jnp.pad(a, ((0, rows - a.shape[0]), (0, 0)))


def kernel(x, positions, ev_norm, ev_w_in, sgu_gain, sgu_w_s, sgu_b, ev_w_out, od_norm, od_w_in,
           gla_w_lr, gla_b_lr, od_w_out, final_norm):
    b, s, d = x.shape
    t = b * s
    x2 = x.reshape(t, d)

    h0 = _rmsnorm(x2, ev_norm[0])
    proj0 = _inproj(h0, ev_w_in[0], ev_w_in.shape[2]).reshape(b, s, -1)
    dk_ret = proj0.shape[2] // 12 // RET_HEADS
    half = dk_ret // 2
    inv = jnp.power(ROPE_BASE, -jnp.arange(half, dtype=F32) / half).reshape(1, half)
    posf = positions.astype(F32).reshape(b, s, 1)
    ret = _retention(proj0, posf, inv)
    sgu = _sgu(proj0, sgu_gain[0], sgu_w_s[0], sgu_b[0])
    w_out0 = ev_w_out[0].astype(BF16)
    kr = ret.shape[2]
    w_in1 = od_w_in[0]
    n_main = w_in1.shape[1] - GLA_RANK
    w_lr_in = jnp.pad(jnp.tile(w_in1[:, n_main:], (1, 3)), ((0, 0), (0, LANES - 3 * GLA_RANK)))
    x1, h1, lr = _outproj_mid(ret.reshape(t, kr), sgu.reshape(t, -1), x2, w_out0[:kr], w_out0[kr:],
                              od_norm[0], w_lr_in.astype(BF16))

    proj1 = _inproj(h1, w_in1, n_main)
    w_hi, w_lo = _split_bf16(gla_w_lr[0])
    wlr3 = _pad_rows(jnp.concatenate([w_hi, w_hi, w_lo], axis=0), LANES)
    mixed = _gla(proj1.reshape(b, s, -1), lr.reshape(b, s, LANES), wlr3, gla_b_lr[0].reshape(1, -1))
    out = _outproj_final(mixed.reshape(t, -1), x1, od_w_out[0].astype(BF16), final_norm)
    return out.reshape(b, s, d)
```

```python
import functools

import numpy as np
import jax
import jax.numpy as jnp
from jax import lax
from jax.experimental import pallas as pl
from jax.experimental.pallas import tpu as pltpu

F32 = jnp.float32
BF16 = jnp.bfloat16

EPS = 1e-6
RET_HEADS = 4
RET_BLOCK = 256
ROPE_BASE = 10000.0
SGU_GROUPS = 8
SGU_CHUNK = 128
GLA_HEADS = 8
GLA_RANK = 16
GLA_TAU = 16.0
GLA_BLOCK = 128
GLA_SAFE_LOG_DECAY = 60.0

LANES = 128
VMEM_LIMIT = 56 * 1024 * 1024

NORM_TM = 512
PROJ_TM = 1024
PROJ_TN = 1024
OUT_TM = 512
RET_TS = 512
SGU_TS = 512
GLA_TS = 512


def _params(*sem):
    return pltpu.CompilerParams(dimension_semantics=sem, vmem_limit_bytes=VMEM_LIMIT)


def _silu(g):
    return g * (1.0 / (1.0 + jnp.exp(-g)))


def _rms(x, gain):
    return x * lax.rsqrt(jnp.mean(x * x, axis=-1, keepdims=True) + EPS) * gain


def _rmsnorm_kernel(x_ref, g_ref, o_ref):
    o_ref[...] = _rms(x_ref[...], g_ref[...]).astype(o_ref.dtype)


def _rmsnorm(x, gain):
    t, d = x.shape
    tm = min(NORM_TM, t)
    return pl.pallas_call(
        _rmsnorm_kernel,
        out_shape=jax.ShapeDtypeStruct((t, d), BF16),
        grid=(t // tm,),
        in_specs=[pl.BlockSpec((tm, d), lambda i: (i, 0)),
                  pl.BlockSpec((1, d), lambda i: (0, 0))],
        out_specs=pl.BlockSpec((tm, d), lambda i: (i, 0)),
        compiler_params=_params("parallel"),
        name="rmsnorm",
    )(x, gain.reshape(1, d))


def _inproj_kernel(h_ref, w_ref, o_ref, wbf_ref):
    @pl.when(pl.program_id(1) == 0)
    def _():
        wbf_ref[...] = w_ref[...].astype(BF16)

    o_ref[...] = jnp.dot(h_ref[...], wbf_ref[...], preferred_element_type=F32).astype(o_ref.dtype)


def _inproj(h, w, n):
    t, d = h.shape
    tm, tn = min(PROJ_TM, t), min(PROJ_TN, n)
    return pl.pallas_call(
        _inproj_kernel,
        out_shape=jax.ShapeDtypeStruct((t, n), BF16),
        grid=(n // tn, t // tm),
        in_specs=[pl.BlockSpec((tm, d), lambda j, i: (i, 0)),
                  pl.BlockSpec((None, d, tn), lambda j, i: (0, 0, j))],
        out_specs=pl.BlockSpec((tm, tn), lambda j, i: (i, j)),
        scratch_shapes=[pltpu.VMEM((d, tn), BF16)],
        compiler_params=_params("parallel", "arbitrary"),
        name="inproj",
    )(h, w)


def _retention_tables(dk):
    c = RET_BLOCK
    h = np.arange(RET_HEADS, dtype=np.float64)
    log_gamma = np.log1p(-np.power(2.0, -5.0 - h))
    idx = np.arange(c, dtype=np.float64)
    rel = idx[:, None] - idx[None, :]
    dmat = np.where(rel >= 0, np.exp(log_gamma[:, None, None] * np.maximum(rel, 0.0)), 0.0)
    qdec = np.exp(log_gamma[:, None] * (idx + 1.0))
    kdec = np.exp(log_gamma[:, None] * (c - 1.0 - idx))
    cdec = np.exp(log_gamma * c)
    lanes = lambda a: np.broadcast_to(a[:, :, None], (RET_HEADS, c, LANES))
    return (jnp.asarray(dmat, F32), jnp.asarray(lanes(qdec), F32), jnp.asarray(lanes(kdec), F32),
            [float(v) for v in cdec], float(dk) ** -0.5)


def _retention_kernel(q_ref, k_ref, v_ref, g_ref, pos_ref, inv_ref, dmat_ref, qdec_ref, kdec_ref,
                      o_ref, state_ref, *, cdec, scale, dk, dv):
    c = RET_BLOCK
    half = dk // 2
    ts = q_ref.shape[1]

    @pl.when(pl.program_id(1) == 0)
    def _():
        state_ref[...] = jnp.zeros_like(state_ref)

    def chunk(ci, carry):
        rows = pl.ds(pl.multiple_of(ci * c, c), c)
        ang = pos_ref[0, rows, :] * inv_ref[...]
        cos, sin = jnp.cos(ang), jnp.sin(ang)
        cos_q, sin_q = cos * scale, sin * scale
        for h in range(RET_HEADS):
            q = q_ref[0, rows, h * dk:(h + 1) * dk].astype(F32)
            k = k_ref[0, rows, h * dk:(h + 1) * dk].astype(F32)
            q1, q2 = q[:, :half], q[:, half:]
            k1, k2 = k[:, :half], k[:, half:]
            qr1, qr2 = q1 * cos_q - q2 * sin_q, q1 * sin_q + q2 * cos_q
            kr1, kr2 = k1 * cos - k2 * sin, k1 * sin + k2 * cos
            qdec, kdec = qdec_ref[h], kdec_ref[h]
            qr = jnp.concatenate([qr1, qr2], axis=-1).astype(BF16)
            kr = jnp.concatenate([kr1, kr2], axis=-1).astype(BF16)
            qd = jnp.concatenate([qr1 * qdec, qr2 * qdec], axis=-1).astype(BF16)
            kd = jnp.concatenate([kr1 * kdec, kr2 * kdec], axis=-1).astype(BF16)
            s = lax.dot_general(qr, kr, (((1,), (1,)), ((), ())), preferred_element_type=F32)
            p = (s * dmat_ref[h]).astype(BF16)
            v = v_ref[0, rows, h * dv:(h + 1) * dv]
            st = state_ref[h]
            o = (jnp.dot(p, v, preferred_element_type=F32)
                 + jnp.dot(qd, st.astype(BF16), preferred_element_type=F32))
            state_ref[h] = cdec[h] * st + lax.dot_general(
                kd, v, (((0,), (0,)), ((), ())), preferred_element_type=F32)
            on = o * lax.rsqrt(jnp.mean(o * o, axis=-1, keepdims=True) + EPS)
            g = g_ref[0, rows, h * dv:(h + 1) * dv]
            o_ref[0, rows, h * dv:(h + 1) * dv] = on.astype(BF16) * _silu(g)
        return carry

    lax.fori_loop(0, ts // c, chunk, 0)


def _retention(proj, posf, inv):
    b, s, _ = proj.shape
    qk_w = proj.shape[2] // 12
    v_w = 2 * qk_w
    dk, dv = qk_w // RET_HEADS, v_w // RET_HEADS
    ts = min(RET_TS, s)
    dmat, qdec, kdec, cdec, scale = _retention_tables(dk)
    const3 = lambda shape: pl.BlockSpec(shape, lambda bi, i: (0, 0, 0))
    return pl.pallas_call(
        functools.partial(_retention_kernel, cdec=cdec, scale=scale, dk=dk, dv=dv),
        out_shape=jax.ShapeDtypeStruct((b, s, v_w), BF16),
        grid=(b, s // ts),
        in_specs=[pl.BlockSpec((1, ts, qk_w), lambda bi, i: (bi, i, 0)),
                  pl.BlockSpec((1, ts, qk_w), lambda bi, i: (bi, i, 1)),
                  pl.BlockSpec((1, ts, v_w), lambda bi, i: (bi, i, 1)),
                  pl.BlockSpec((1, ts, v_w), lambda bi, i: (bi, i, 2)),
                  pl.BlockSpec((1, ts, 1), lambda bi, i: (bi, i, 0)),
                  pl.BlockSpec((1, dk // 2), lambda bi, i: (0, 0)),
                  const3(dmat.shape), const3(qdec.shape), const3(kdec.shape)],
        out_specs=pl.BlockSpec((1, ts, v_w), lambda bi, i: (bi, i, 0)),
        scratch_shapes=[pltpu.VMEM((RET_HEADS, dk, dv), F32)],
        compiler_params=_params("parallel", "arbitrary"),
        name="retention",
    )(proj, proj, proj, proj, posf, inv, dmat, qdec, kdec)


def _sgu_kernel(u_ref, v_ref, g_ref, gain_ref, w_ref, b_ref, o_ref, *, dg):
    c = SGU_CHUNK
    ts = u_ref.shape[1]
    row = lax.broadcasted_iota(jnp.int32, (c, c), 0)
    col = lax.broadcasted_iota(jnp.int32, (c, c), 1)
    causal = row >= col
    wm = [jnp.where(causal, w_ref[g], 0.0).astype(BF16) for g in range(SGU_GROUPS)]

    def chunk(ci, carry):
        rows = pl.ds(pl.multiple_of(ci * c, c), c)
        for g in range(SGU_GROUPS):
            cols = slice(g * dg, (g + 1) * dg)
            v = v_ref[0, rows, cols].astype(F32)
            d = v - jnp.mean(v, axis=-1, keepdims=True)
            var = jnp.mean(d * d, axis=-1, keepdims=True)
            vn = d * lax.rsqrt(var + EPS) * gain_ref[:, cols]
            sg = jnp.dot(wm[g], vn.astype(BF16), preferred_element_type=F32) + b_ref[g]
            o_ref[0, rows, cols] = sg.astype(BF16) * (u_ref[0, rows, cols] * _silu(g_ref[0, rows, cols]))
        return carry

    lax.fori_loop(0, ts // c, chunk, 0)


def _sgu(proj, gain, w_s, b_s):
    b, s, _ = proj.shape
    w = proj.shape[2] // 6
    dg = w // SGU_GROUPS
    ts = min(SGU_TS, s)
    bias = jnp.broadcast_to(b_s[:, :, None], (SGU_GROUPS, SGU_CHUNK, dg))
    const3 = lambda shape: pl.BlockSpec(shape, lambda bi, i: (0, 0, 0))
    return pl.pallas_call(
        functools.partial(_sgu_kernel, dg=dg),
        out_shape=jax.ShapeDtypeStruct((b, s, w), BF16),
        grid=(b, s // ts),
        in_specs=[pl.BlockSpec((1, ts, w), lambda bi, i: (bi, i, 3)),
                  pl.BlockSpec((1, ts, w), lambda bi, i: (bi, i, 4)),
                  pl.BlockSpec((1, ts, w), lambda bi, i: (bi, i, 5)),
                  pl.BlockSpec((1, w), lambda bi, i: (0, 0)),
                  const3(w_s.shape), const3(bias.shape)],
        out_specs=pl.BlockSpec((1, ts, w), lambda bi, i: (bi, i, 0)),
        compiler_params=_params("parallel", "parallel"),
        name="sgu",
    )(proj, proj, proj, gain.reshape(1, w), w_s, bias)


def _outproj_mid_kernel(a_ref, b_ref, x_ref, wa_ref, wb_ref, gn_ref, wlr_ref, x1_ref, h1_ref, lr_ref):
    acc = (jnp.dot(a_ref[...], wa_ref[...], preferred_element_type=F32)
           + jnp.dot(b_ref[...], wb_ref[...], preferred_element_type=F32))
    x1 = x_ref[...] + acc
    x1_ref[...] = x1
    h1 = _rms(x1, gn_ref[...]).astype(h1_ref.dtype)
    h1_ref[...] = h1
    lr_ref[...] = jnp.dot(h1, wlr_ref[...], preferred_element_type=F32)


def _outproj_final_kernel(a_ref, x_ref, w_ref, gn_ref, o_ref):
    x1 = x_ref[...] + jnp.dot(a_ref[...], w_ref[...], preferred_element_type=F32)
    o_ref[...] = _rms(x1, gn_ref[...])


def _resident(shape):
    return pl.BlockSpec(shape, lambda i: (0, 0), pipeline_mode=pl.Buffered(1))


def _outproj_mid(a, b, x, wa, wb, gain, wlr):
    t, d = x.shape
    ka, kb = a.shape[1], b.shape[1]
    tm = min(OUT_TM, t)
    row = lambda w: pl.BlockSpec((tm, w), lambda i: (i, 0))
    return pl.pallas_call(
        _outproj_mid_kernel,
        out_shape=(jax.ShapeDtypeStruct((t, d), F32), jax.ShapeDtypeStruct((t, d), BF16),
                   jax.ShapeDtypeStruct((t, LANES), F32)),
        grid=(t // tm,),
        in_specs=[row(ka), row(kb), row(d), _resident((ka, d)), _resident((kb, d)), _resident((1, d)),
                  _resident((d, LANES))],
        out_specs=(row(d), row(d), row(LANES)),
        compiler_params=_params("parallel"),
        name="outproj_mid",
    )(a, b, x, wa, wb, gain.reshape(1, d), wlr)


def _outproj_final(a, x, w, gain):
    t, d = x.shape
    ka = a.shape[1]
    tm = min(OUT_TM, t)
    row = lambda w_: pl.BlockSpec((tm, w_), lambda i: (i, 0))
    return pl.pallas_call(
        _outproj_final_kernel,
        out_shape=jax.ShapeDtypeStruct((t, d), F32),
        grid=(t // tm,),
        in_specs=[row(ka), row(d), _resident((ka, d)), _resident((1, d))],
        out_specs=row(d),
        compiler_params=_params("parallel"),
        name="outproj_final",
    )(a, x, w, gain.reshape(1, d))


def _split_bf16(x):
    hi = x.astype(BF16)
    lo = (x - hi.astype(F32)).astype(BF16)
    return hi, lo


def _gla_kernel(q_ref, k_ref, v_ref, g_ref, lr_ref, wlr_ref, blr_ref, o_ref,
                state_ref, state0_ref, bc_ref, kf_ref, *, scale):
    c = GLA_BLOCK
    ts = q_ref.shape[1]
    n_blocks = ts // c

    @pl.when(pl.program_id(2) == 0)
    def _():
        state_ref[...] = jnp.zeros_like(state_ref)

    state0_ref[...] = state_ref[...]

    lr_hi, lr_lo = _split_bf16(lr_ref[0])
    lane = lax.broadcasted_iota(jnp.int32, lr_hi.shape, 1)
    lr3 = jnp.where((lane >= GLA_RANK) & (lane < 2 * GLA_RANK), lr_lo, lr_hi)
    z = jnp.dot(lr3, wlr_ref[...], preferred_element_type=F32) + blr_ref[...]
    log_alpha = (jnp.minimum(z, 0.0) - jnp.log(1.0 + jnp.exp(-jnp.abs(z)))) * (1.0 / GLA_TAU)

    row = lax.broadcasted_iota(jnp.int32, (c, c), 0)
    col = lax.broadcasted_iota(jnp.int32, (c, c), 1)
    causal = row >= col
    tri = jnp.where(causal, 1.0, 0.0).astype(BF16)
    tri2 = jnp.concatenate([tri, tri], axis=1)

    worst = None
    for ci in range(n_blocks):
        rows = slice(ci * c, (ci + 1) * c)
        la_hi, la_lo = _split_bf16(log_alpha[rows])
        bc = jnp.dot(tri2, jnp.concatenate([la_hi, la_lo], axis=0), preferred_element_type=F32)
        bc_ref[rows, :] = bc
        worst = bc[c - 1:c, :] if worst is None else jnp.minimum(worst, bc[c - 1:c, :])

    def factored_scores(ci, bc, q, k, q_dec, k_dec):
        mid = bc[c // 2 - 1:c // 2, :]
        last = bc[c - 1:c, :]
        q_t = (q_dec * jnp.exp(-mid)).astype(BF16)
        k_t = (k_dec * jnp.exp(mid - last)).astype(BF16)
        s = lax.dot_general(q_t, k_t, (((1,), (1,)), ((), ())), preferred_element_type=F32)
        return jnp.where(causal, s, 0.0).astype(BF16)

    def exact_scores(ci, bc, q, k, q_dec, k_dec):
        kf_ref[...] = k

        def column(s, p):
            w = jnp.exp(jnp.minimum(bc - bc_ref[pl.ds(ci * c + s, 1), :], 0.0))
            val = jnp.sum(q * w * kf_ref[pl.ds(s, 1), :], axis=-1, keepdims=True)
            return jnp.where(col == s, val, p)

        p = lax.fori_loop(0, c, column, jnp.zeros((c, c), F32))
        return jnp.where(causal, p, 0.0).astype(BF16)

    def run_blocks(scores_fn):
        for ci in range(n_blocks):
            rows = slice(ci * c, (ci + 1) * c)
            bc = bc_ref[rows, :]
            last = bc[c - 1:c, :]
            q = q_ref[0, rows, :].astype(F32)
            k = k_ref[0, rows, :].astype(F32)
            v = v_ref[0, rows, :]
            q_dec = q * jnp.exp(bc)
            k_dec = k * jnp.exp(last - bc)
            q_in, k_out = q_dec.astype(BF16), k_dec.astype(BF16)
            p = scores_fn(ci, bc, q, k, q_dec, k_dec)
            st = state_ref[...]
            o = (jnp.dot(p, v, preferred_element_type=F32)
                 + lax.dot_general(q_in, st.astype(BF16), (((1,), (1,)), ((), ())),
                                   preferred_element_type=F32))
            state_ref[...] = st * jnp.exp(last) + lax.dot_general(
                v, k_out, (((0,), (0,)), ((), ())), preferred_element_type=F32)
            on = o * lax.rsqrt(jnp.mean(o * o, axis=-1, keepdims=True) + EPS / (scale * scale))
            o_ref[0, rows, :] = on.astype(BF16) * _silu(g_ref[0, rows, :])

    run_blocks(factored_scores)

    @pl.when(jnp.logical_not(jnp.min(worst) > -GLA_SAFE_LOG_DECAY))
    def _():
        state_ref[...] = state0_ref[...]
        run_blocks(exact_scores)


def _gla(proj, lr, wlr3, blr):
    b, s, _ = proj.shape
    qk_w = proj.shape[2] // 6
    v_w = 2 * qk_w
    dk, dv = qk_w // GLA_HEADS, v_w // GLA_HEADS
    ts = min(GLA_TS, s)
    nh = GLA_HEADS
    return pl.pallas_call(
        functools.partial(_gla_kernel, scale=float(dk) ** -0.5),
        out_shape=jax.ShapeDtypeStruct((b, s, v_w), BF16),
        grid=(b, nh, s // ts),
        in_specs=[pl.BlockSpec((1, ts, dk), lambda bi, h, i: (bi, i, h)),
                  pl.BlockSpec((1, ts, dk), lambda bi, h, i: (bi, i, nh + h)),
                  pl.BlockSpec((1, ts, dv), lambda bi, h, i: (bi, i, nh + h)),
                  pl.BlockSpec((1, ts, dv), lambda bi, h, i: (bi, i, 2 * nh + h)),
                  pl.BlockSpec((1, ts, LANES), lambda bi, h, i: (bi, i, 0)),
                  pl.BlockSpec((LANES, dk), lambda bi, h, i: (0, h)),
                  pl.BlockSpec((1, dk), lambda bi, h, i: (0, h))],
        out_specs=pl.BlockSpec((1, ts, dv), lambda bi, h, i: (bi, i, h)),
        scratch_shapes=[pltpu.VMEM((dv, dk), F32),
                        pltpu.VMEM((dv, dk), F32),
                        pltpu.VMEM((ts, dk), F32),
                        pltpu.VMEM((GLA_BLOCK, dk), F32)],
        compiler_params=_params("parallel", "parallel", "arbitrary"),
        name="gla",
    )(proj, proj, proj, proj, lr, wlr3, blr)


def _pad_rows(a, rows):
    return jnp.pad(a, ((0, rows - a.shape[0]), (0, 0)))


def kernel(x, positions, ev_norm, ev_w_in, sgu_gain, sgu_w_s, sgu_b, ev_w_out, od_norm, od_w_in,
           gla_w_lr, gla_b_lr, od_w_out, final_norm):
    b, s, d = x.shape
    t = b * s
    x2 = x.reshape(t, d)

    h0 = _rmsnorm(x2, ev_norm[0])
    proj0 = _inproj(h0, ev_w_in, ev_w_in.shape[2]).reshape(b, s, -1)
    dk_ret = proj0.shape[2] // 12 // RET_HEADS
    half = dk_ret // 2
    inv = jnp.power(ROPE_BASE, -jnp.arange(half, dtype=F32) / half).reshape(1, half)
    posf = positions.astype(F32).reshape(b, s, 1)
    ret = _retention(proj0, posf, inv)
    sgu = _sgu(proj0, sgu_gain[0], sgu_w_s[0], sgu_b[0])
    w_out0 = ev_w_out[0].astype(BF16)
    kr = ret.shape[2]
    n_main = od_w_in.shape[2] - GLA_RANK
    w_lr_in = jnp.pad(jnp.tile(od_w_in[0, :, n_main:], (1, 3)), ((0, 0), (0, LANES - 3 * GLA_RANK)))
    x1, h1, lr = _outproj_mid(ret.reshape(t, kr), sgu.reshape(t, -1), x2, w_out0[:kr], w_out0[kr:],
                              od_norm[0], w_lr_in.astype(BF16))

    proj1 = _inproj(h1, od_w_in, n_main)
    w_hi, w_lo = _split_bf16(gla_w_lr[0])
    wlr3 = _pad_rows(jnp.concatenate([w_hi, w_hi, w_lo], axis=0), LANES)
    mixed = _gla(proj1.reshape(b, s, -1), lr.reshape(b, s, LANES), wlr3, gla_b_lr[0].reshape(1, -1))
    out = _outproj_final(mixed.reshape(t, -1), x1, od_w_out[0].astype(BF16), final_norm)
    return out.reshape(b, s, d)
```

```python
import functools

import numpy as np
import jax
import jax.numpy as jnp
from jax import lax
from jax.experimental import pallas as pl
from jax.experimental.pallas import tpu as pltpu

F32 = jnp.float32
BF16 = jnp.bfloat16

EPS = 1e-6
RET_HEADS = 4
RET_BLOCK = 256
ROPE_BASE = 10000.0
SGU_GROUPS = 8
SGU_CHUNK = 128
GLA_HEADS = 8
GLA_RANK = 16
GLA_TAU = 16.0
GLA_BLOCK = 128
GLA_SAFE_LOG_DECAY = 60.0

LANES = 128
VMEM_LIMIT = 56 * 1024 * 1024

NORM_TM = 512
PROJ_TM = 1024
PROJ_TN = 1024
OUT_TM = 512
RET_TS = 512
SGU_TS = 512
GLA_TS = 512


def _params(*sem):
    return pltpu.CompilerParams(dimension_semantics=sem, vmem_limit_bytes=VMEM_LIMIT)


def _silu(g):
    return g * (1.0 / (1.0 + jnp.exp(-g)))


def _rms(x, gain):
    return x * lax.rsqrt(jnp.mean(x * x, axis=-1, keepdims=True) + EPS) * gain


def _rmsnorm_kernel(x_ref, g_ref, o_ref):
    o_ref[...] = _rms(x_ref[...], g_ref[...]).astype(o_ref.dtype)


def _rmsnorm(x, gain):
    t, d = x.shape
    tm = min(NORM_TM, t)
    return pl.pallas_call(
        _rmsnorm_kernel,
        out_shape=jax.ShapeDtypeStruct((t, d), BF16),
        grid=(t // tm,),
        in_specs=[pl.BlockSpec((tm, d), lambda i: (i, 0)),
                  pl.BlockSpec((1, d), lambda i: (0, 0))],
        out_specs=pl.BlockSpec((tm, d), lambda i: (i, 0)),
        compiler_params=_params("parallel"),
        name="rmsnorm",
    )(x, gain.reshape(1, d))


def _inproj_kernel(h_ref, w_ref, o_ref, wbf_ref, *, w_transposed):
    @pl.when(pl.program_id(1) == 0)
    def _():
        wbf_ref[...] = w_ref[...].astype(BF16)

    contract_w = 1 if w_transposed else 0
    o_ref[...] = lax.dot_general(h_ref[...], wbf_ref[...], (((1,), (contract_w,)), ((), ())),
                                 preferred_element_type=F32).astype(o_ref.dtype)


def _inproj(h, w, n, w_transposed=False):
    t, d = h.shape
    tm, tn = min(PROJ_TM, t), min(PROJ_TN, n)
    if w_transposed:
        w_spec = pl.BlockSpec((None, tn, d), lambda j, i: (0, j, 0))
    else:
        w_spec = pl.BlockSpec((None, d, tn), lambda j, i: (0, 0, j))
    return pl.pallas_call(
        functools.partial(_inproj_kernel, w_transposed=w_transposed),
        out_shape=jax.ShapeDtypeStruct((t, n), BF16),
        grid=(n // tn, t // tm),
        in_specs=[pl.BlockSpec((tm, d), lambda j, i: (i, 0)), w_spec],
        out_specs=pl.BlockSpec((tm, tn), lambda j, i: (i, j)),
        scratch_shapes=[pltpu.VMEM((tn, d) if w_transposed else (d, tn), BF16)],
        compiler_params=_params("parallel", "arbitrary"),
        name="inproj",
    )(h, w)


def _retention_tables(dk):
    c = RET_BLOCK
    h = np.arange(RET_HEADS, dtype=np.float64)
    log_gamma = np.log1p(-np.power(2.0, -5.0 - h))
    idx = np.arange(c, dtype=np.float64)
    rel = idx[:, None] - idx[None, :]
    dmat = np.where(rel >= 0, np.exp(log_gamma[:, None, None] * np.maximum(rel, 0.0)), 0.0)
    qdec = np.exp(log_gamma[:, None] * (idx + 1.0))
    kdec = np.exp(log_gamma[:, None] * (c - 1.0 - idx))
    cdec = np.exp(log_gamma * c)
    lanes = lambda a: np.broadcast_to(a[:, :, None], (RET_HEADS, c, LANES))
    return (jnp.asarray(dmat, F32), jnp.asarray(lanes(qdec), F32), jnp.asarray(lanes(kdec), F32),
            [float(v) for v in cdec], float(dk) ** -0.5)


def _retention_kernel(q_ref, k_ref, v_ref, g_ref, pos_ref, inv_ref, dmat_ref, qdec_ref, kdec_ref,
                      o_ref, state_ref, *, cdec, scale, dk, dv):
    c = RET_BLOCK
    half = dk // 2
    ts = q_ref.shape[1]

    @pl.when(pl.program_id(1) == 0)
    def _():
        state_ref[...] = jnp.zeros_like(state_ref)

    def chunk(ci, carry):
        rows = pl.ds(pl.multiple_of(ci * c, c), c)
        ang = pos_ref[0, rows, :] * inv_ref[...]
        cos, sin = jnp.cos(ang), jnp.sin(ang)
        cos_q, sin_q = cos * scale, sin * scale
        for h in range(RET_HEADS):
            q = q_ref[0, rows, h * dk:(h + 1) * dk].astype(F32)
            k = k_ref[0, rows, h * dk:(h + 1) * dk].astype(F32)
            q1, q2 = q[:, :half], q[:, half:]
            k1, k2 = k[:, :half], k[:, half:]
            qr1, qr2 = q1 * cos_q - q2 * sin_q, q1 * sin_q + q2 * cos_q
            kr1, kr2 = k1 * cos - k2 * sin, k1 * sin + k2 * cos
            qdec, kdec = qdec_ref[h], kdec_ref[h]
            qr = jnp.concatenate([qr1, qr2], axis=-1).astype(BF16)
            kr = jnp.concatenate([kr1, kr2], axis=-1).astype(BF16)
            qd = jnp.concatenate([qr1 * qdec, qr2 * qdec], axis=-1).astype(BF16)
            kd = jnp.concatenate([kr1 * kdec, kr2 * kdec], axis=-1).astype(BF16)
            s = lax.dot_general(qr, kr, (((1,), (1,)), ((), ())), preferred_element_type=F32)
            p = (s * dmat_ref[h]).astype(BF16)
            v = v_ref[0, rows, h * dv:(h + 1) * dv]
            st = state_ref[h]
            o = (jnp.dot(p, v, preferred_element_type=F32)
                 + jnp.dot(qd, st.astype(BF16), preferred_element_type=F32))
            state_ref[h] = cdec[h] * st + lax.dot_general(
                kd, v, (((0,), (0,)), ((), ())), preferred_element_type=F32)
            on = o * lax.rsqrt(jnp.mean(o * o, axis=-1, keepdims=True) + EPS)
            g = g_ref[0, rows, h * dv:(h + 1) * dv]
            o_ref[0, rows, h * dv:(h + 1) * dv] = on.astype(BF16) * _silu(g)
        return carry

    lax.fori_loop(0, ts // c, chunk, 0)


def _retention(proj, posf, inv):
    b, s, _ = proj.shape
    qk_w = proj.shape[2] // 12
    v_w = 2 * qk_w
    dk, dv = qk_w // RET_HEADS, v_w // RET_HEADS
    ts = min(RET_TS, s)
    dmat, qdec, kdec, cdec, scale = _retention_tables(dk)
    const3 = lambda shape: pl.BlockSpec(shape, lambda bi, i: (0, 0, 0))
    return pl.pallas_call(
        functools.partial(_retention_kernel, cdec=cdec, scale=scale, dk=dk, dv=dv),
        out_shape=jax.ShapeDtypeStruct((b, s, v_w), BF16),
        grid=(b, s // ts),
        in_specs=[pl.BlockSpec((1, ts, qk_w), lambda bi, i: (bi, i, 0)),
                  pl.BlockSpec((1, ts, qk_w), lambda bi, i: (bi, i, 1)),
                  pl.BlockSpec((1, ts, v_w), lambda bi, i: (bi, i, 1)),
                  pl.BlockSpec((1, ts, v_w), lambda bi, i: (bi, i, 2)),
                  pl.BlockSpec((1, ts, 1), lambda bi, i: (bi, i, 0)),
                  pl.BlockSpec((1, dk // 2), lambda bi, i: (0, 0)),
                  const3(dmat.shape), const3(qdec.shape), const3(kdec.shape)],
        out_specs=pl.BlockSpec((1, ts, v_w), lambda bi, i: (bi, i, 0)),
        scratch_shapes=[pltpu.VMEM((RET_HEADS, dk, dv), F32)],
        compiler_params=_params("parallel", "arbitrary"),
        name="retention",
    )(proj, proj, proj, proj, posf, inv, dmat, qdec, kdec)


def _sgu_kernel(u_ref, v_ref, g_ref, gain_ref, w_ref, b_ref, o_ref, *, dg):
    c = SGU_CHUNK
    ts = u_ref.shape[1]
    row = lax.broadcasted_iota(jnp.int32, (c, c), 0)
    col = lax.broadcasted_iota(jnp.int32, (c, c), 1)
    causal = row >= col
    wm = [jnp.where(causal, w_ref[g], 0.0).astype(BF16) for g in range(SGU_GROUPS)]

    def chunk(ci, carry):
        rows = pl.ds(pl.multiple_of(ci * c, c), c)
        for g in range(SGU_GROUPS):
            cols = slice(g * dg, (g + 1) * dg)
            v = v_ref[0, rows, cols].astype(F32)
            d = v - jnp.mean(v, axis=-1, keepdims=True)
            var = jnp.mean(d * d, axis=-1, keepdims=True)
            vn = d * lax.rsqrt(var + EPS) * gain_ref[:, cols]
            sg = jnp.dot(wm[g], vn.astype(BF16), preferred_element_type=F32) + b_ref[g]
            o_ref[0, rows, cols] = sg.astype(BF16) * (u_ref[0, rows, cols] * _silu(g_ref[0, rows, cols]))
        return carry

    lax.fori_loop(0, ts // c, chunk, 0)


def _sgu(proj, gain, w_s, b_s):
    b, s, _ = proj.shape
    w = proj.shape[2] // 6
    dg = w // SGU_GROUPS
    ts = min(SGU_TS, s)
    bias = jnp.broadcast_to(b_s[:, :, None], (SGU_GROUPS, SGU_CHUNK, dg))
    const3 = lambda shape: pl.BlockSpec(shape, lambda bi, i: (0, 0, 0))
    return pl.pallas_call(
        functools.partial(_sgu_kernel, dg=dg),
        out_shape=jax.ShapeDtypeStruct((b, s, w), BF16),
        grid=(b, s // ts),
        in_specs=[pl.BlockSpec((1, ts, w), lambda bi, i: (bi, i, 3)),
                  pl.BlockSpec((1, ts, w), lambda bi, i: (bi, i, 4)),
                  pl.BlockSpec((1, ts, w), lambda bi, i: (bi, i, 5)),
                  pl.BlockSpec((1, w), lambda bi, i: (0, 0)),
                  const3(w_s.shape), const3(bias.shape)],
        out_specs=pl.BlockSpec((1, ts, w), lambda bi, i: (bi, i, 0)),
        compiler_params=_params("parallel", "parallel"),
        name="sgu",
    )(proj, proj, proj, gain.reshape(1, w), w_s, bias)


def _outproj_mid_kernel(a_ref, b_ref, x_ref, wa_ref, wb_ref, gn_ref, wlr_ref, x1_ref, h1_ref, lr_ref):
    acc = (jnp.dot(a_ref[...], wa_ref[...], preferred_element_type=F32)
           + jnp.dot(b_ref[...], wb_ref[...], preferred_element_type=F32))
    x1 = x_ref[...] + acc
    x1_ref[...] = x1
    h1 = _rms(x1, gn_ref[...]).astype(h1_ref.dtype)
    h1_ref[...] = h1
    lr_ref[...] = jnp.dot(h1, wlr_ref[...], preferred_element_type=F32)


def _outproj_final_kernel(a_ref, x_ref, w_ref, gn_ref, o_ref):
    x1 = x_ref[...] + jnp.dot(a_ref[...], w_ref[...], preferred_element_type=F32)
    o_ref[...] = _rms(x1, gn_ref[...])


def _resident(shape):
    return pl.BlockSpec(shape, lambda i: (0, 0), pipeline_mode=pl.Buffered(1))


def _outproj_mid(a, b, x, w, gain, wlr):
    t, d = x.shape
    ka, kb = a.shape[1], b.shape[1]
    assert ka == kb and w.shape == (ka + kb, d)
    tm = min(OUT_TM, t)
    row = lambda w_: pl.BlockSpec((tm, w_), lambda i: (i, 0))
    w_half = lambda blk: pl.BlockSpec((ka, d), lambda i: (blk, 0), pipeline_mode=pl.Buffered(1))
    return pl.pallas_call(
        _outproj_mid_kernel,
        out_shape=(jax.ShapeDtypeStruct((t, d), F32), jax.ShapeDtypeStruct((t, d), BF16),
                   jax.ShapeDtypeStruct((t, LANES), F32)),
        grid=(t // tm,),
        in_specs=[row(ka), row(kb), row(d), w_half(0), w_half(1), _resident((1, d)),
                  _resident((d, LANES))],
        out_specs=(row(d), row(d), row(LANES)),
        compiler_params=_params("parallel"),
        name="outproj_mid",
    )(a, b, x, w, w, gain.reshape(1, d), wlr)


def _outproj_final(a, x, w, gain):
    t, d = x.shape
    ka = a.shape[1]
    tm = min(OUT_TM, t)
    row = lambda w_: pl.BlockSpec((tm, w_), lambda i: (i, 0))
    return pl.pallas_call(
        _outproj_final_kernel,
        out_shape=jax.ShapeDtypeStruct((t, d), F32),
        grid=(t // tm,),
        in_specs=[row(ka), row(d), _resident((ka, d)), _resident((1, d))],
        out_specs=row(d),
        compiler_params=_params("parallel"),
        name="outproj_final",
    )(a, x, w, gain.reshape(1, d))


def _split_bf16(x):
    hi = x.astype(BF16)
    lo = (x - hi.astype(F32)).astype(BF16)
    return hi, lo


def _gla_kernel(q_ref, k_ref, v_ref, g_ref, lr_ref, wlr_ref, blr_ref, o_ref,
                state_ref, state0_ref, bc_ref, kf_ref, *, scale):
    c = GLA_BLOCK
    ts = q_ref.shape[1]
    n_blocks = ts // c

    @pl.when(pl.program_id(2) == 0)
    def _():
        state_ref[...] = jnp.zeros_like(state_ref)

    state0_ref[...] = state_ref[...]

    lr_hi, lr_lo = _split_bf16(lr_ref[0])
    lane = lax.broadcasted_iota(jnp.int32, lr_hi.shape, 1)
    lr3 = jnp.where((lane >= GLA_RANK) & (lane < 2 * GLA_RANK), lr_lo, lr_hi)
    z = jnp.dot(lr3, wlr_ref[...], preferred_element_type=F32) + blr_ref[...]
    log_alpha = (jnp.minimum(z, 0.0) - jnp.log(1.0 + jnp.exp(-jnp.abs(z)))) * (1.0 / GLA_TAU)

    row = lax.broadcasted_iota(jnp.int32, (c, c), 0)
    col = lax.broadcasted_iota(jnp.int32, (c, c), 1)
    causal = row >= col
    tri = jnp.where(causal, 1.0, 0.0).astype(BF16)
    tri2 = jnp.concatenate([tri, tri], axis=1)

    worst = None
    for ci in range(n_blocks):
        rows = slice(ci * c, (ci + 1) * c)
        la_hi, la_lo = _split_bf16(log_alpha[rows])
        bc = jnp.dot(tri2, jnp.concatenate([la_hi, la_lo], axis=0), preferred_element_type=F32)
        bc_ref[rows, :] = bc
        worst = bc[c - 1:c, :] if worst is None else jnp.minimum(worst, bc[c - 1:c, :])

    def factored_scores(ci, bc, q, k, q_dec, k_dec):
        mid = bc[c // 2 - 1:c // 2, :]
        last = bc[c - 1:c, :]
        q_t = (q_dec * jnp.exp(-mid)).astype(BF16)
        k_t = (k_dec * jnp.exp(mid - last)).astype(BF16)
        s = lax.dot_general(q_t, k_t, (((1,), (1,)), ((), ())), preferred_element_type=F32)
        return jnp.where(causal, s, 0.0).astype(BF16)

    def exact_scores(ci, bc, q, k, q_dec, k_dec):
        kf_ref[...] = k

        def column(s, p):
            w = jnp.exp(jnp.minimum(bc - bc_ref[pl.ds(ci * c + s, 1), :], 0.0))
            val = jnp.sum(q * w * kf_ref[pl.ds(s, 1), :], axis=-1, keepdims=True)
            return jnp.where(col == s, val, p)

        p = lax.fori_loop(0, c, column, jnp.zeros((c, c), F32))
        return jnp.where(causal, p, 0.0).astype(BF16)

    def run_blocks(scores_fn):
        local = []
        for ci in range(n_blocks):
            rows = slice(ci * c, (ci + 1) * c)
            bc = bc_ref[rows, :]
            last = bc[c - 1:c, :]
            q = q_ref[0, rows, :].astype(F32)
            k = k_ref[0, rows, :].astype(F32)
            v = v_ref[0, rows, :]
            q_dec = q * jnp.exp(bc)
            k_dec = k * jnp.exp(last - bc)
            p = scores_fn(ci, bc, q, k, q_dec, k_dec)
            intra = jnp.dot(p, v, preferred_element_type=F32)
            update = lax.dot_general(v, k_dec.astype(BF16), (((0,), (0,)), ((), ())),
                                     preferred_element_type=F32)
            local.append((q_dec.astype(BF16), intra, update, jnp.exp(last)))
        for ci, (q_in, intra, update, decay) in enumerate(local):
            rows = slice(ci * c, (ci + 1) * c)
            st = state_ref[...]
            o = intra + lax.dot_general(q_in, st.astype(BF16), (((1,), (1,)), ((), ())),
                                        preferred_element_type=F32)
            state_ref[...] = st * decay + update
            on = o * lax.rsqrt(jnp.mean(o * o, axis=-1, keepdims=True) + EPS / (scale * scale))
            o_ref[0, rows, :] = on.astype(BF16) * _silu(g_ref[0, rows, :])

    run_blocks(factored_scores)

    @pl.when(jnp.logical_not(jnp.min(worst) > -GLA_SAFE_LOG_DECAY))
    def _():
        state_ref[...] = state0_ref[...]
        run_blocks(exact_scores)


def _gla(proj, lr, wlr3, blr):
    b, s, _ = proj.shape
    qk_w = proj.shape[2] // 6
    v_w = 2 * qk_w
    dk, dv = qk_w // GLA_HEADS, v_w // GLA_HEADS
    ts = min(GLA_TS, s)
    nh = GLA_HEADS
    return pl.pallas_call(
        functools.partial(_gla_kernel, scale=float(dk) ** -0.5),
        out_shape=jax.ShapeDtypeStruct((b, s, v_w), BF16),
        grid=(b, nh, s // ts),
        in_specs=[pl.BlockSpec((1, ts, dk), lambda bi, h, i: (bi, i, h)),
                  pl.BlockSpec((1, ts, dk), lambda bi, h, i: (bi, i, nh + h)),
                  pl.BlockSpec((1, ts, dv), lambda bi, h, i: (bi, i, nh + h)),
                  pl.BlockSpec((1, ts, dv), lambda bi, h, i: (bi, i, 2 * nh + h)),
                  pl.BlockSpec((1, ts, LANES), lambda bi, h, i: (bi, i, 0)),
                  pl.BlockSpec((LANES, dk), lambda bi, h, i: (0, h)),
                  pl.BlockSpec((1, dk), lambda bi, h, i: (0, h))],
        out_specs=pl.BlockSpec((1, ts, dv), lambda bi, h, i: (bi, i, h)),
        scratch_shapes=[pltpu.VMEM((dv, dk), F32),
                        pltpu.VMEM((dv, dk), F32),
                        pltpu.VMEM((ts, dk), F32),
                        pltpu.VMEM((GLA_BLOCK, dk), F32)],
        compiler_params=_params("parallel", "parallel", "arbitrary"),
        name="gla",
    )(proj, proj, proj, proj, lr, wlr3, blr)


def _pad_rows(a, rows):
    return jnp.pad(a, ((0, rows - a.shape[0]), (0, 0)))


def kernel(x, positions, ev_norm, ev_w_in, sgu_gain, sgu_w_s, sgu_b, ev_w_out, od_norm, od_w_in,
           gla_w_lr, gla_b_lr, od_w_out, final_norm):
    b, s, d = x.shape
    t = b * s
    x2 = x.reshape(t, d)

    h0 = _rmsnorm(x2, ev_norm[0])
    proj0 = _inproj(h0, ev_w_in, ev_w_in.shape[2]).reshape(b, s, -1)
    dk_ret = proj0.shape[2] // 12 // RET_HEADS
    half = dk_ret // 2
    inv = jnp.power(ROPE_BASE, -jnp.arange(half, dtype=F32) / half).reshape(1, half)
    posf = positions.astype(F32).reshape(b, s, 1)
    ret = _retention(proj0, posf, inv)
    sgu = _sgu(proj0, sgu_gain[0], sgu_w_s[0], sgu_b[0])
    w_out0 = ev_w_out[0].astype(BF16)
    kr = ret.shape[2]
    n_main = od_w_in.shape[2] - GLA_RANK
    w_lr_in = jnp.pad(jnp.tile(od_w_in[0, :, n_main:], (1, 3)), ((0, 0), (0, LANES - 3 * GLA_RANK)))
    x1, h1, lr = _outproj_mid(ret.reshape(t, kr), sgu.reshape(t, -1), x2, w_out0, od_norm[0],
                              w_lr_in.astype(BF16))

    proj1 = _inproj(h1, jnp.swapaxes(od_w_in, 1, 2), n_main, w_transposed=True)
    w_hi, w_lo = _split_bf16(gla_w_lr[0])
    wlr3 = _pad_rows(jnp.concatenate([w_hi, w_hi, w_lo], axis=0), LANES)
    mixed = _gla(proj1.reshape(b, s, -1), lr.reshape(b, s, LANES), wlr3, gla_b_lr[0].reshape(1, -1))
    out = _outproj_final(mixed.reshape(t, -1), x1, od_w_out[0].astype(BF16), final_norm)
    return out.reshape(b, s, d)
```

```python
import functools

import numpy as np
import jax
import jax.numpy as jnp
from jax import lax
from jax.experimental import pallas as pl
from jax.experimental.pallas import tpu as pltpu

F32 = jnp.float32
BF16 = jnp.bfloat16

EPS = 1e-6
RET_HEADS = 4
RET_BLOCK = 256
ROPE_BASE = 10000.0
SGU_GROUPS = 8
SGU_CHUNK = 128
GLA_HEADS = 8
GLA_RANK = 16
GLA_TAU = 16.0
GLA_BLOCK = 128
GLA_SAFE_LOG_DECAY = 60.0

LANES = 128
VMEM_LIMIT = 56 * 1024 * 1024

NORM_TM = 512
PROJ_TM = 1024
PROJ_TN = 1024
OUT_TM = 512
RET_TS = 512
MID_TM = 256
GLA_TS = 1024


def _params(*sem):
    return pltpu.CompilerParams(dimension_semantics=sem, vmem_limit_bytes=VMEM_LIMIT)


def _silu(g):
    return g * (1.0 / (1.0 + jnp.exp(-g)))


def _rms(x, gain):
    return x * lax.rsqrt(jnp.mean(x * x, axis=-1, keepdims=True) + EPS) * gain


def _rmsnorm_kernel(x_ref, g_ref, o_ref):
    o_ref[...] = _rms(x_ref[...], g_ref[...]).astype(o_ref.dtype)


def _rmsnorm(x, gain):
    t, d = x.shape
    tm = min(NORM_TM, t)
    return pl.pallas_call(
        _rmsnorm_kernel,
        out_shape=jax.ShapeDtypeStruct((t, d), BF16),
        grid=(t // tm,),
        in_specs=[pl.BlockSpec((tm, d), lambda i: (i, 0)),
                  pl.BlockSpec((1, d), lambda i: (0, 0))],
        out_specs=pl.BlockSpec((tm, d), lambda i: (i, 0)),
        compiler_params=_params("parallel"),
        name="rmsnorm",
    )(x, gain.reshape(1, d))


def _inproj_kernel(h_ref, w_ref, o_ref, wbf_ref, *, w_transposed):
    @pl.when(pl.program_id(1) == 0)
    def _():
        wbf_ref[...] = w_ref[...].astype(BF16)

    contract_w = 1 if w_transposed else 0
    o_ref[...] = lax.dot_general(h_ref[...], wbf_ref[...], (((1,), (contract_w,)), ((), ())),
                                 preferred_element_type=F32).astype(o_ref.dtype)


def _inproj(h, w, n, w_transposed=False):
    t, d = h.shape
    tm, tn = min(PROJ_TM, t), min(PROJ_TN, n)
    if w_transposed:
        w_spec = pl.BlockSpec((None, tn, d), lambda j, i: (0, j, 0))
    else:
        w_spec = pl.BlockSpec((None, d, tn), lambda j, i: (0, 0, j))
    return pl.pallas_call(
        functools.partial(_inproj_kernel, w_transposed=w_transposed),
        out_shape=jax.ShapeDtypeStruct((t, n), BF16),
        grid=(n // tn, t // tm),
        in_specs=[pl.BlockSpec((tm, d), lambda j, i: (i, 0)), w_spec],
        out_specs=pl.BlockSpec((tm, tn), lambda j, i: (i, j)),
        scratch_shapes=[pltpu.VMEM((tn, d) if w_transposed else (d, tn), BF16)],
        compiler_params=_params("parallel", "arbitrary"),
        name="inproj",
    )(h, w)


def _retention_tables(dk):
    c = RET_BLOCK
    h = np.arange(RET_HEADS, dtype=np.float64)
    log_gamma = np.log1p(-np.power(2.0, -5.0 - h))
    idx = np.arange(c, dtype=np.float64)
    rel = idx[:, None] - idx[None, :]
    dmat = np.where(rel >= 0, np.exp(log_gamma[:, None, None] * np.maximum(rel, 0.0)), 0.0)
    qdec = np.exp(log_gamma[:, None] * (idx + 1.0))
    kdec = np.exp(log_gamma[:, None] * (c - 1.0 - idx))
    cdec = np.exp(log_gamma * c)
    lanes = lambda a: np.broadcast_to(a[:, :, None], (RET_HEADS, c, LANES))
    return (jnp.asarray(dmat, F32), jnp.asarray(lanes(qdec), F32), jnp.asarray(lanes(kdec), F32),
            [float(v) for v in cdec], float(dk) ** -0.5)


def _retention_kernel(q_ref, k_ref, v_ref, g_ref, pos_ref, inv_ref, dmat_ref, qdec_ref, kdec_ref,
                      o_ref, state_ref, *, cdec, scale, dk, dv):
    c = RET_BLOCK
    half = dk // 2
    ts = q_ref.shape[1]

    @pl.when(pl.program_id(1) == 0)
    def _():
        state_ref[...] = jnp.zeros_like(state_ref)

    def chunk(ci, carry):
        rows = pl.ds(pl.multiple_of(ci * c, c), c)
        ang = pos_ref[0, rows, :] * inv_ref[...]
        cos, sin = jnp.cos(ang), jnp.sin(ang)
        cos_q, sin_q = cos * scale, sin * scale
        for h in range(RET_HEADS):
            q = q_ref[0, rows, h * dk:(h + 1) * dk].astype(F32)
            k = k_ref[0, rows, h * dk:(h + 1) * dk].astype(F32)
            q1, q2 = q[:, :half], q[:, half:]
            k1, k2 = k[:, :half], k[:, half:]
            qr1, qr2 = q1 * cos_q - q2 * sin_q, q1 * sin_q + q2 * cos_q
            kr1, kr2 = k1 * cos - k2 * sin, k1 * sin + k2 * cos
            qdec, kdec = qdec_ref[h], kdec_ref[h]
            qr = jnp.concatenate([qr1, qr2], axis=-1).astype(BF16)
            kr = jnp.concatenate([kr1, kr2], axis=-1).astype(BF16)
            qd = jnp.concatenate([qr1 * qdec, qr2 * qdec], axis=-1).astype(BF16)
            kd = jnp.concatenate([kr1 * kdec, kr2 * kdec], axis=-1).astype(BF16)
            s = lax.dot_general(qr, kr, (((1,), (1,)), ((), ())), preferred_element_type=F32)
            p = (s * dmat_ref[h]).astype(BF16)
            v = v_ref[0, rows, h * dv:(h + 1) * dv]
            st = state_ref[h]
            o = (jnp.dot(p, v, preferred_element_type=F32)
                 + jnp.dot(qd, st.astype(BF16), preferred_element_type=F32))
            state_ref[h] = cdec[h] * st + lax.dot_general(
                kd, v, (((0,), (0,)), ((), ())), preferred_element_type=F32)
            on = o * lax.rsqrt(jnp.mean(o * o, axis=-1, keepdims=True) + EPS)
            g = g_ref[0, rows, h * dv:(h + 1) * dv]
            o_ref[0, rows, h * dv:(h + 1) * dv] = on.astype(BF16) * _silu(g)
        return carry

    lax.fori_loop(0, ts // c, chunk, 0)


def _retention(proj, posf, inv):
    b, s, _ = proj.shape
    qk_w = proj.shape[2] // 12
    v_w = 2 * qk_w
    dk, dv = qk_w // RET_HEADS, v_w // RET_HEADS
    ts = min(RET_TS, s)
    dmat, qdec, kdec, cdec, scale = _retention_tables(dk)
    const3 = lambda shape: pl.BlockSpec(shape, lambda bi, i: (0, 0, 0))
    return pl.pallas_call(
        functools.partial(_retention_kernel, cdec=cdec, scale=scale, dk=dk, dv=dv),
        out_shape=jax.ShapeDtypeStruct((b, s, v_w), BF16),
        grid=(b, s // ts),
        in_specs=[pl.BlockSpec((1, ts, qk_w), lambda bi, i: (bi, i, 0)),
                  pl.BlockSpec((1, ts, qk_w), lambda bi, i: (bi, i, 1)),
                  pl.BlockSpec((1, ts, v_w), lambda bi, i: (bi, i, 1)),
                  pl.BlockSpec((1, ts, v_w), lambda bi, i: (bi, i, 2)),
                  pl.BlockSpec((1, ts, 1), lambda bi, i: (bi, i, 0)),
                  pl.BlockSpec((1, dk // 2), lambda bi, i: (0, 0)),
                  const3(dmat.shape), const3(qdec.shape), const3(kdec.shape)],
        out_specs=pl.BlockSpec((1, ts, v_w), lambda bi, i: (bi, i, 0)),
        scratch_shapes=[pltpu.VMEM((RET_HEADS, dk, dv), F32)],
        compiler_params=_params("parallel", "arbitrary"),
        name="retention",
    )(proj, proj, proj, proj, posf, inv, dmat, qdec, kdec)


def _sgu_block(u_ref, v_ref, g_ref, gain_ref, ws_ref, sb_ref, sgu_ref, *, dg):
    c = SGU_CHUNK
    row = lax.broadcasted_iota(jnp.int32, (c, c), 0)
    col = lax.broadcasted_iota(jnp.int32, (c, c), 1)
    causal = row >= col
    wm = [jnp.where(causal, ws_ref[g], 0.0).astype(BF16) for g in range(SGU_GROUPS)]
    for ci in range(u_ref.shape[0] // c):
        rows = slice(ci * c, (ci + 1) * c)
        for g in range(SGU_GROUPS):
            cols = slice(g * dg, (g + 1) * dg)
            v = v_ref[rows, cols].astype(F32)
            d = v - jnp.mean(v, axis=-1, keepdims=True)
            var = jnp.mean(d * d, axis=-1, keepdims=True)
            vn = d * lax.rsqrt(var + EPS) * gain_ref[:, cols]
            sg = jnp.dot(wm[g], vn.astype(BF16), preferred_element_type=F32) + sb_ref[g]
            sgu_ref[rows, cols] = sg.astype(BF16) * (u_ref[rows, cols] * _silu(g_ref[rows, cols]))


def _outproj_mid_kernel(a_ref, u_ref, v_ref, g_ref, x_ref, wa_ref, wb_ref, gn_ref, wlr_ref,
                        sgain_ref, ws_ref, sb_ref, x1_ref, h1_ref, lr_ref, sgu_ref, *, dg):
    acc = jnp.dot(a_ref[...], wa_ref[...], preferred_element_type=F32)
    _sgu_block(u_ref, v_ref, g_ref, sgain_ref, ws_ref, sb_ref, sgu_ref, dg=dg)
    acc = acc + jnp.dot(sgu_ref[...], wb_ref[...], preferred_element_type=F32)
    x1 = x_ref[...] + acc
    x1_ref[...] = x1
    h1 = _rms(x1, gn_ref[...]).astype(h1_ref.dtype)
    h1_ref[...] = h1
    lr_ref[...] = jnp.dot(h1, wlr_ref[...], preferred_element_type=F32)


def _outproj_final_kernel(a_ref, x_ref, w_ref, gn_ref, o_ref):
    x1 = x_ref[...] + jnp.dot(a_ref[...], w_ref[...], preferred_element_type=F32)
    o_ref[...] = _rms(x1, gn_ref[...])


def _resident(shape):
    return pl.BlockSpec(shape, lambda i: (0, 0), pipeline_mode=pl.Buffered(1))


def _outproj_mid(a, proj, x, w, gain, wlr, sgu_gain, w_s, b_s):
    t, d = x.shape
    ka = a.shape[1]
    kb = proj.shape[1] // 6
    assert ka == kb and w.shape == (ka + kb, d)
    dg = kb // SGU_GROUPS
    tm = min(MID_TM, t)
    bias = jnp.broadcast_to(b_s[:, :, None], (SGU_GROUPS, SGU_CHUNK, dg))
    row = lambda w_, blk=0: pl.BlockSpec((tm, w_), lambda i: (i, blk))
    w_half = lambda blk: pl.BlockSpec((ka, d), lambda i: (blk, 0), pipeline_mode=pl.Buffered(1))
    const3 = lambda shape: pl.BlockSpec(shape, lambda i: (0, 0, 0), pipeline_mode=pl.Buffered(1))
    return pl.pallas_call(
        functools.partial(_outproj_mid_kernel, dg=dg),
        out_shape=(jax.ShapeDtypeStruct((t, d), F32), jax.ShapeDtypeStruct((t, d), BF16),
                   jax.ShapeDtypeStruct((t, LANES), F32)),
        grid=(t // tm,),
        in_specs=[row(ka), row(kb, 3), row(kb, 4), row(kb, 5), row(d), w_half(0), w_half(1),
                  _resident((1, d)), _resident((d, LANES)), _resident((1, kb)),
                  const3(w_s.shape), const3(bias.shape)],
        out_specs=(row(d), row(d), row(LANES)),
        scratch_shapes=[pltpu.VMEM((tm, kb), BF16)],
        compiler_params=_params("parallel"),
        name="outproj_mid",
    )(a, proj, proj, proj, x, w, w, gain.reshape(1, d), wlr, sgu_gain.reshape(1, kb), w_s, bias)


def _outproj_final(a, x, w, gain):
    t, d = x.shape
    ka = a.shape[1]
    tm = min(OUT_TM, t)
    row = lambda w_: pl.BlockSpec((tm, w_), lambda i: (i, 0))
    return pl.pallas_call(
        _outproj_final_kernel,
        out_shape=jax.ShapeDtypeStruct((t, d), F32),
        grid=(t // tm,),
        in_specs=[row(ka), row(d), _resident((ka, d)), _resident((1, d))],
        out_specs=row(d),
        compiler_params=_params("parallel"),
        name="outproj_final",
    )(a, x, w, gain.reshape(1, d))


def _split_bf16(x):
    hi = x.astype(BF16)
    lo = (x - hi.astype(F32)).astype(BF16)
    return hi, lo


def _gla_kernel(q_ref, k_ref, v_ref, g_ref, lr_ref, wlr_ref, blr_ref, o_ref,
                state_ref, state0_ref, bc_ref, kf_ref, *, scale):
    c = GLA_BLOCK
    ts = q_ref.shape[1]
    n_blocks = ts // c

    @pl.when(pl.program_id(2) == 0)
    def _():
        state_ref[...] = jnp.zeros_like(state_ref)

    state0_ref[...] = state_ref[...]

    lr_hi, lr_lo = _split_bf16(lr_ref[0])
    lane = lax.broadcasted_iota(jnp.int32, lr_hi.shape, 1)
    lr3 = jnp.where((lane >= GLA_RANK) & (lane < 2 * GLA_RANK), lr_lo, lr_hi)
    z = jnp.dot(lr3, wlr_ref[...], preferred_element_type=F32) + blr_ref[...]
    log_alpha = (jnp.minimum(z, 0.0) - jnp.log(1.0 + jnp.exp(-jnp.abs(z)))) * (1.0 / GLA_TAU)

    row = lax.broadcasted_iota(jnp.int32, (c, c), 0)
    col = lax.broadcasted_iota(jnp.int32, (c, c), 1)
    causal = row >= col
    tri = jnp.where(causal, 1.0, 0.0).astype(BF16)
    tri2 = jnp.concatenate([tri, tri], axis=1)

    worst = None
    for ci in range(n_blocks):
        rows = slice(ci * c, (ci + 1) * c)
        la_hi, la_lo = _split_bf16(log_alpha[rows])
        bc = jnp.dot(tri2, jnp.concatenate([la_hi, la_lo], axis=0), preferred_element_type=F32)
        bc_ref[rows, :] = bc
        worst = bc[c - 1:c, :] if worst is None else jnp.minimum(worst, bc[c - 1:c, :])

    def factored_scores(ci, bc, q, k, q_dec, k_dec):
        mid = bc[c // 2 - 1:c // 2, :]
        last = bc[c - 1:c, :]
        q_t = (q_dec * jnp.exp(-mid)).astype(BF16)
        k_t = (k_dec * jnp.exp(mid - last)).astype(BF16)
        s = lax.dot_general(q_t, k_t, (((1,), (1,)), ((), ())), preferred_element_type=F32)
        return jnp.where(causal, s, 0.0).astype(BF16)

    def exact_scores(ci, bc, q, k, q_dec, k_dec):
        kf_ref[...] = k

        def column(s, p):
            w = jnp.exp(jnp.minimum(bc - bc_ref[pl.ds(ci * c + s, 1), :], 0.0))
            val = jnp.sum(q * w * kf_ref[pl.ds(s, 1), :], axis=-1, keepdims=True)
            return jnp.where(col == s, val, p)

        p = lax.fori_loop(0, c, column, jnp.zeros((c, c), F32))
        return jnp.where(causal, p, 0.0).astype(BF16)

    def run_blocks(scores_fn):
        local = []
        for ci in range(n_blocks):
            rows = slice(ci * c, (ci + 1) * c)
            bc = bc_ref[rows, :]
            last = bc[c - 1:c, :]
            q = q_ref[0, rows, :].astype(F32)
            k = k_ref[0, rows, :].astype(F32)
            v = v_ref[0, rows, :]
            q_dec = q * jnp.exp(bc)
            k_dec = k * jnp.exp(last - bc)
            p = scores_fn(ci, bc, q, k, q_dec, k_dec)
            intra = jnp.dot(p, v, preferred_element_type=F32)
            update = lax.dot_general(v, k_dec.astype(BF16), (((0,), (0,)), ((), ())),
                                     preferred_element_type=F32)
            local.append((q_dec.astype(BF16), intra, update, jnp.exp(last)))
        for ci, (q_in, intra, update, decay) in enumerate(local):
            rows = slice(ci * c, (ci + 1) * c)
            st = state_ref[...]
            o = intra + lax.dot_general(q_in, st.astype(BF16), (((1,), (1,)), ((), ())),
                                        preferred_element_type=F32)
            state_ref[...] = st * decay + update
            on = o * lax.rsqrt(jnp.mean(o * o, axis=-1, keepdims=True) + EPS / (scale * scale))
            o_ref[0, rows, :] = on.astype(BF16) * _silu(g_ref[0, rows, :])

    run_blocks(factored_scores)

    @pl.when(jnp.logical_not(jnp.min(worst) > -GLA_SAFE_LOG_DECAY))
    def _():
        state_ref[...] = state0_ref[...]
        run_blocks(exact_scores)


def _gla(proj, lr, wlr3, blr):
    b, s, _ = proj.shape
    qk_w = proj.shape[2] // 6
    v_w = 2 * qk_w
    dk, dv = qk_w // GLA_HEADS, v_w // GLA_HEADS
    ts = min(GLA_TS, s)
    nh = GLA_HEADS
    return pl.pallas_call(
        functools.partial(_gla_kernel, scale=float(dk) ** -0.5),
        out_shape=jax.ShapeDtypeStruct((b, s, v_w), BF16),
        grid=(b, nh, s // ts),
        in_specs=[pl.BlockSpec((1, ts, dk), lambda bi, h, i: (bi, i, h)),
                  pl.BlockSpec((1, ts, dk), lambda bi, h, i: (bi, i, nh + h)),
                  pl.BlockSpec((1, ts, dv), lambda bi, h, i: (bi, i, nh + h)),
                  pl.BlockSpec((1, ts, dv), lambda bi, h, i: (bi, i, 2 * nh + h)),
                  pl.BlockSpec((1, ts, LANES), lambda bi, h, i: (bi, i, 0)),
                  pl.BlockSpec((LANES, dk), lambda bi, h, i: (0, h)),
                  pl.BlockSpec((1, dk), lambda bi, h, i: (0, h))],
        out_specs=pl.BlockSpec((1, ts, dv), lambda bi, h, i: (bi, i, h)),
        scratch_shapes=[pltpu.VMEM((dv, dk), F32),
                        pltpu.VMEM((dv, dk), F32),
                        pltpu.VMEM((ts, dk), F32),
                        pltpu.VMEM((GLA_BLOCK, dk), F32)],
        compiler_params=_params("parallel", "parallel", "arbitrary"),
        name="gla",
    )(proj, proj, proj, proj, lr, wlr3, blr)


def _pad_rows(a, rows):
    return jnp.pad(a, ((0, rows - a.shape[0]), (0, 0)))


def kernel(x, positions, ev_norm, ev_w_in, sgu_gain, sgu_w_s, sgu_b, ev_w_out, od_norm, od_w_in,
           gla_w_lr, gla_b_lr, od_w_out, final_norm):
    b, s, d = x.shape
    t = b * s
    x2 = x.reshape(t, d)

    h0 = _rmsnorm(x2, ev_norm[0])
    proj0 = _inproj(h0, ev_w_in, ev_w_in.shape[2]).reshape(b, s, -1)
    dk_ret = proj0.shape[2] // 12 // RET_HEADS
    half = dk_ret // 2
    inv = jnp.power(ROPE_BASE, -jnp.arange(half, dtype=F32) / half).reshape(1, half)
    posf = positions.astype(F32).reshape(b, s, 1)
    ret = _retention(proj0, posf, inv)
    w_out0 = ev_w_out[0].astype(BF16)
    kr = ret.shape[2]
    n_main = od_w_in.shape[2] - GLA_RANK
    w_lr_in = jnp.pad(jnp.tile(od_w_in[0, :, n_main:], (1, 3)), ((0, 0), (0, LANES - 3 * GLA_RANK)))
    x1, h1, lr = _outproj_mid(ret.reshape(t, kr), proj0.reshape(t, -1), x2, w_out0, od_norm[0],
                              w_lr_in.astype(BF16), sgu_gain[0], sgu_w_s[0], sgu_b[0])

    proj1 = _inproj(h1, jnp.swapaxes(od_w_in, 1, 2), n_main, w_transposed=True)
    w_hi, w_lo = _split_bf16(gla_w_lr[0])
    wlr3 = _pad_rows(jnp.concatenate([w_hi, w_hi, w_lo], axis=0), LANES)
    mixed = _gla(proj1.reshape(b, s, -1), lr.reshape(b, s, LANES), wlr3, gla_b_lr[0].reshape(1, -1))
    out = _outproj_final(mixed.reshape(t, -1), x1, od_w_out[0].astype(BF16), final_norm)
    return out.reshape(b, s, d)
```

```python
import functools

import numpy as np
import jax
import jax.numpy as jnp
from jax import lax
from jax.experimental import pallas as pl
from jax.experimental.pallas import tpu as pltpu

F32 = jnp.float32
BF16 = jnp.bfloat16

EPS = 1e-6
RET_HEADS = 4
RET_BLOCK = 256
ROPE_BASE = 10000.0
SGU_GROUPS = 8
SGU_CHUNK = 128
GLA_HEADS = 8
GLA_RANK = 16
GLA_TAU = 16.0
GLA_BLOCK = 128
GLA_SAFE_LOG_DECAY = 60.0

LANES = 128
VMEM_LIMIT = 56 * 1024 * 1024

NORM_TM = 512
PROJ_TM = 1024
PROJ_TN = 1024
OUT_TM = 256
RET_TS = 512
MID_TM = 256
GLA_TS = 1024


def _params(*sem):
    return pltpu.CompilerParams(dimension_semantics=sem, vmem_limit_bytes=VMEM_LIMIT)


def _silu(g):
    return g * (1.0 / (1.0 + jnp.exp(-g)))


def _rms(x, gain):
    return x * lax.rsqrt(jnp.mean(x * x, axis=-1, keepdims=True) + EPS) * gain


def _rmsnorm_kernel(x_ref, g_ref, o_ref):
    o_ref[...] = _rms(x_ref[...], g_ref[...]).astype(o_ref.dtype)


def _rmsnorm(x, gain):
    t, d = x.shape
    tm = min(NORM_TM, t)
    return pl.pallas_call(
        _rmsnorm_kernel,
        out_shape=jax.ShapeDtypeStruct((t, d), BF16),
        grid=(t // tm,),
        in_specs=[pl.BlockSpec((tm, d), lambda i: (i, 0)),
                  pl.BlockSpec((1, d), lambda i: (0, 0))],
        out_specs=pl.BlockSpec((tm, d), lambda i: (i, 0)),
        compiler_params=_params("parallel"),
        name="rmsnorm",
    )(x, gain.reshape(1, d))


def _inproj_kernel(h_ref, w_ref, o_ref, wbf_ref, *, w_transposed):
    @pl.when(pl.program_id(1) == 0)
    def _():
        wbf_ref[...] = w_ref[...].astype(BF16)

    contract_w = 1 if w_transposed else 0
    o_ref[...] = lax.dot_general(h_ref[...], wbf_ref[...], (((1,), (contract_w,)), ((), ())),
                                 preferred_element_type=F32).astype(o_ref.dtype)


def _inproj(h, w, n, w_transposed=False):
    t, d = h.shape
    tm, tn = min(PROJ_TM, t), min(PROJ_TN, n)
    if w_transposed:
        w_spec = pl.BlockSpec((None, tn, d), lambda j, i: (0, j, 0))
    else:
        w_spec = pl.BlockSpec((None, d, tn), lambda j, i: (0, 0, j))
    return pl.pallas_call(
        functools.partial(_inproj_kernel, w_transposed=w_transposed),
        out_shape=jax.ShapeDtypeStruct((t, n), BF16),
        grid=(n // tn, t // tm),
        in_specs=[pl.BlockSpec((tm, d), lambda j, i: (i, 0)), w_spec],
        out_specs=pl.BlockSpec((tm, tn), lambda j, i: (i, j)),
        scratch_shapes=[pltpu.VMEM((tn, d) if w_transposed else (d, tn), BF16)],
        compiler_params=_params("parallel", "arbitrary"),
        name="inproj",
    )(h, w)


def _retention_tables(dk):
    c = RET_BLOCK
    h = np.arange(RET_HEADS, dtype=np.float64)
    log_gamma = np.log1p(-np.power(2.0, -5.0 - h))
    idx = np.arange(c, dtype=np.float64)
    rel = idx[:, None] - idx[None, :]
    dmat = np.where(rel >= 0, np.exp(log_gamma[:, None, None] * np.maximum(rel, 0.0)), 0.0)
    qdec = np.exp(log_gamma[:, None] * (idx + 1.0))
    kdec = np.exp(log_gamma[:, None] * (c - 1.0 - idx))
    cdec = np.exp(log_gamma * c)
    lanes = lambda a: np.broadcast_to(a[:, :, None], (RET_HEADS, c, LANES))
    return (jnp.asarray(dmat, F32), jnp.asarray(lanes(qdec), F32), jnp.asarray(lanes(kdec), F32),
            [float(v) for v in cdec], float(dk) ** -0.5)


def _retention_kernel(q_ref, k_ref, v_ref, g_ref, pos_ref, inv_ref, dmat_ref, qdec_ref, kdec_ref,
                      o_ref, state_ref, *, cdec, scale, dk, dv):
    c = RET_BLOCK
    half = dk // 2
    ts = q_ref.shape[1]

    @pl.when(pl.program_id(1) == 0)
    def _():
        state_ref[...] = jnp.zeros_like(state_ref)

    def chunk(ci, carry):
        rows = pl.ds(pl.multiple_of(ci * c, c), c)
        ang = pos_ref[0, rows, :] * inv_ref[...]
        cos, sin = jnp.cos(ang), jnp.sin(ang)
        cos_q, sin_q = cos * scale, sin * scale
        for h in range(RET_HEADS):
            q = q_ref[0, rows, h * dk:(h + 1) * dk].astype(F32)
            k = k_ref[0, rows, h * dk:(h + 1) * dk].astype(F32)
            q1, q2 = q[:, :half], q[:, half:]
            k1, k2 = k[:, :half], k[:, half:]
            qr1, qr2 = q1 * cos_q - q2 * sin_q, q1 * sin_q + q2 * cos_q
            kr1, kr2 = k1 * cos - k2 * sin, k1 * sin + k2 * cos
            qdec, kdec = qdec_ref[h], kdec_ref[h]
            qr = jnp.concatenate([qr1, qr2], axis=-1).astype(BF16)
            kr = jnp.concatenate([kr1, kr2], axis=-1).astype(BF16)
            qd = jnp.concatenate([qr1 * qdec, qr2 * qdec], axis=-1).astype(BF16)
            kd = jnp.concatenate([kr1 * kdec, kr2 * kdec], axis=-1).astype(BF16)
            s = lax.dot_general(qr, kr, (((1,), (1,)), ((), ())), preferred_element_type=F32)
            p = (s * dmat_ref[h]).astype(BF16)
            v = v_ref[0, rows, h * dv:(h + 1) * dv]
            st = state_ref[h]
            o = (jnp.dot(p, v, preferred_element_type=F32)
                 + jnp.dot(qd, st.astype(BF16), preferred_element_type=F32))
            state_ref[h] = cdec[h] * st + lax.dot_general(
                kd, v, (((0,), (0,)), ((), ())), preferred_element_type=F32)
            on = o * lax.rsqrt(jnp.mean(o * o, axis=-1, keepdims=True) + EPS)
            g = g_ref[0, rows, h * dv:(h + 1) * dv]
            o_ref[0, rows, h * dv:(h + 1) * dv] = on.astype(BF16) * _silu(g)
        return carry

    lax.fori_loop(0, ts // c, chunk, 0)


def _retention(proj, posf, inv):
    b, s, _ = proj.shape
    qk_w = proj.shape[2] // 12
    v_w = 2 * qk_w
    dk, dv = qk_w // RET_HEADS, v_w // RET_HEADS
    ts = min(RET_TS, s)
    dmat, qdec, kdec, cdec, scale = _retention_tables(dk)
    const3 = lambda shape: pl.BlockSpec(shape, lambda bi, i: (0, 0, 0))
    return pl.pallas_call(
        functools.partial(_retention_kernel, cdec=cdec, scale=scale, dk=dk, dv=dv),
        out_shape=jax.ShapeDtypeStruct((b, s, v_w), BF16),
        grid=(b, s // ts),
        in_specs=[pl.BlockSpec((1, ts, qk_w), lambda bi, i: (bi, i, 0)),
                  pl.BlockSpec((1, ts, qk_w), lambda bi, i: (bi, i, 1)),
                  pl.BlockSpec((1, ts, v_w), lambda bi, i: (bi, i, 1)),
                  pl.BlockSpec((1, ts, v_w), lambda bi, i: (bi, i, 2)),
                  pl.BlockSpec((1, ts, 1), lambda bi, i: (bi, i, 0)),
                  pl.BlockSpec((1, dk // 2), lambda bi, i: (0, 0)),
                  const3(dmat.shape), const3(qdec.shape), const3(kdec.shape)],
        out_specs=pl.BlockSpec((1, ts, v_w), lambda bi, i: (bi, i, 0)),
        scratch_shapes=[pltpu.VMEM((RET_HEADS, dk, dv), F32)],
        compiler_params=_params("parallel", "arbitrary"),
        name="retention",
    )(proj, proj, proj, proj, posf, inv, dmat, qdec, kdec)


def _sgu_block(u_ref, v_ref, g_ref, gain_ref, ws_ref, sb_ref, sgu_ref, *, dg):
    c = SGU_CHUNK
    row = lax.broadcasted_iota(jnp.int32, (c, c), 0)
    col = lax.broadcasted_iota(jnp.int32, (c, c), 1)
    causal = row >= col
    wm = [jnp.where(causal, ws_ref[g], 0.0).astype(BF16) for g in range(SGU_GROUPS)]
    for ci in range(u_ref.shape[0] // c):
        rows = slice(ci * c, (ci + 1) * c)
        for g in range(SGU_GROUPS):
            cols = slice(g * dg, (g + 1) * dg)
            v = v_ref[rows, cols].astype(F32)
            d = v - jnp.mean(v, axis=-1, keepdims=True)
            var = jnp.mean(d * d, axis=-1, keepdims=True)
            vn = d * lax.rsqrt(var + EPS) * gain_ref[:, cols]
            sg = jnp.dot(wm[g], vn.astype(BF16), preferred_element_type=F32) + sb_ref[g]
            sgu_ref[rows, cols] = sg.astype(BF16) * (u_ref[rows, cols] * _silu(g_ref[rows, cols]))


def _outproj_mid_kernel(a_ref, u_ref, v_ref, g_ref, x_ref, wa_ref, wb_ref, gn_ref, wlr_ref,
                        sgain_ref, ws_ref, sb_ref, x1_ref, h1_ref, lr_ref, sgu_ref, *, dg):
    acc = jnp.dot(a_ref[...], wa_ref[...], preferred_element_type=F32)
    _sgu_block(u_ref, v_ref, g_ref, sgain_ref, ws_ref, sb_ref, sgu_ref, dg=dg)
    acc = acc + jnp.dot(sgu_ref[...], wb_ref[...], preferred_element_type=F32)
    x1 = x_ref[...] + acc
    x1_ref[...] = x1
    h1 = _rms(x1, gn_ref[...]).astype(h1_ref.dtype)
    h1_ref[...] = h1
    lr_ref[...] = jnp.dot(h1, wlr_ref[...], preferred_element_type=F32)


def _outproj_final_kernel(a_ref, g_ref, x_ref, w_ref, gn_ref, o_ref, *, dv, head_eps):
    acc = x_ref[...]
    for h in range(a_ref.shape[1] // dv):
        cols = slice(h * dv, (h + 1) * dv)
        o = a_ref[:, cols].astype(F32)
        on = o * lax.rsqrt(jnp.mean(o * o, axis=-1, keepdims=True) + head_eps)
        mixed = on.astype(BF16) * _silu(g_ref[:, cols])
        acc = acc + jnp.dot(mixed, w_ref[cols, :], preferred_element_type=F32)
    o_ref[...] = _rms(acc, gn_ref[...])


def _resident(shape):
    return pl.BlockSpec(shape, lambda i: (0, 0), pipeline_mode=pl.Buffered(1))


def _outproj_mid(a, proj, x, w, gain, wlr, sgu_gain, w_s, b_s):
    t, d = x.shape
    ka = a.shape[1]
    kb = proj.shape[1] // 6
    assert ka == kb and w.shape == (ka + kb, d)
    dg = kb // SGU_GROUPS
    tm = min(MID_TM, t)
    bias = jnp.broadcast_to(b_s[:, :, None], (SGU_GROUPS, SGU_CHUNK, dg))
    row = lambda w_, blk=0: pl.BlockSpec((tm, w_), lambda i: (i, blk))
    w_half = lambda blk: pl.BlockSpec((ka, d), lambda i: (blk, 0), pipeline_mode=pl.Buffered(1))
    const3 = lambda shape: pl.BlockSpec(shape, lambda i: (0, 0, 0), pipeline_mode=pl.Buffered(1))
    return pl.pallas_call(
        functools.partial(_outproj_mid_kernel, dg=dg),
        out_shape=(jax.ShapeDtypeStruct((t, d), F32), jax.ShapeDtypeStruct((t, d), BF16),
                   jax.ShapeDtypeStruct((t, LANES), F32)),
        grid=(t // tm,),
        in_specs=[row(ka), row(kb, 3), row(kb, 4), row(kb, 5), row(d), w_half(0), w_half(1),
                  _resident((1, d)), _resident((d, LANES)), _resident((1, kb)),
                  const3(w_s.shape), const3(bias.shape)],
        out_specs=(row(d), row(d), row(LANES)),
        scratch_shapes=[pltpu.VMEM((tm, kb), BF16)],
        compiler_params=_params("parallel"),
        name="outproj_mid",
    )(a, proj, proj, proj, x, w, w, gain.reshape(1, d), wlr, sgu_gain.reshape(1, kb), w_s, bias)


def _outproj_final(a, proj, x, w, gain, dv, head_eps):
    t, d = x.shape
    ka = a.shape[1]
    tm = min(OUT_TM, t)
    row = lambda w_, blk=0: pl.BlockSpec((tm, w_), lambda i: (i, blk))
    return pl.pallas_call(
        functools.partial(_outproj_final_kernel, dv=dv, head_eps=head_eps),
        out_shape=jax.ShapeDtypeStruct((t, d), F32),
        grid=(t // tm,),
        in_specs=[row(ka), row(ka, 2), row(d), _resident((ka, d)), _resident((1, d))],
        out_specs=row(d),
        compiler_params=_params("parallel"),
        name="outproj_final",
    )(a, proj, x, w, gain.reshape(1, d))


def _split_bf16(x):
    hi = x.astype(BF16)
    lo = (x - hi.astype(F32)).astype(BF16)
    return hi, lo


def _gla_kernel(q_ref, k_ref, v_ref, lr_ref, wlr_ref, blr_ref, o_ref,
                state_ref, state0_ref, bc_ref, kf_ref):
    c = GLA_BLOCK
    ts = q_ref.shape[1]
    n_blocks = ts // c

    @pl.when(pl.program_id(2) == 0)
    def _():
        state_ref[...] = jnp.zeros_like(state_ref)

    state0_ref[...] = state_ref[...]

    lr_hi, lr_lo = _split_bf16(lr_ref[0])
    lane = lax.broadcasted_iota(jnp.int32, lr_hi.shape, 1)
    lr3 = jnp.where((lane >= GLA_RANK) & (lane < 2 * GLA_RANK), lr_lo, lr_hi)
    z = jnp.dot(lr3, wlr_ref[...], preferred_element_type=F32) + blr_ref[...]
    log_alpha = (jnp.minimum(z, 0.0) - jnp.log(1.0 + jnp.exp(-jnp.abs(z)))) * (1.0 / GLA_TAU)

    row = lax.broadcasted_iota(jnp.int32, (c, c), 0)
    col = lax.broadcasted_iota(jnp.int32, (c, c), 1)
    causal = row >= col
    tri = jnp.where(causal, 1.0, 0.0).astype(BF16)
    tri2 = jnp.concatenate([tri, tri], axis=1)

    worst = None
    for ci in range(n_blocks):
        rows = slice(ci * c, (ci + 1) * c)
        la_hi, la_lo = _split_bf16(log_alpha[rows])
        bc = jnp.dot(tri2, jnp.concatenate([la_hi, la_lo], axis=0), preferred_element_type=F32)
        bc_ref[rows, :] = bc
        worst = bc[c - 1:c, :] if worst is None else jnp.minimum(worst, bc[c - 1:c, :])

    def factored_scores(ci, bc, q, k, q_dec, k_dec):
        mid = bc[c // 2 - 1:c // 2, :]
        last = bc[c - 1:c, :]
        q_t = (q_dec * jnp.exp(-mid)).astype(BF16)
        k_t = (k_dec * jnp.exp(mid - last)).astype(BF16)
        s = lax.dot_general(q_t, k_t, (((1,), (1,)), ((), ())), preferred_element_type=F32)
        return jnp.where(causal, s, 0.0).astype(BF16)

    def exact_scores(ci, bc, q, k, q_dec, k_dec):
        kf_ref[...] = k

        def column(s, p):
            w = jnp.exp(jnp.minimum(bc - bc_ref[pl.ds(ci * c + s, 1), :], 0.0))
            val = jnp.sum(q * w * kf_ref[pl.ds(s, 1), :], axis=-1, keepdims=True)
            return jnp.where(col == s, val, p)

        p = lax.fori_loop(0, c, column, jnp.zeros((c, c), F32))
        return jnp.where(causal, p, 0.0).astype(BF16)

    def run_blocks(scores_fn):
        local = []
        for ci in range(n_blocks):
            rows = slice(ci * c, (ci + 1) * c)
            bc = bc_ref[rows, :]
            last = bc[c - 1:c, :]
            q = q_ref[0, rows, :].astype(F32)
            k = k_ref[0, rows, :].astype(F32)
            v = v_ref[0, rows, :]
            q_dec = q * jnp.exp(bc)
            k_dec = k * jnp.exp(last - bc)
            p = scores_fn(ci, bc, q, k, q_dec, k_dec)
            intra = jnp.dot(p, v, preferred_element_type=F32)
            update = lax.dot_general(v, k_dec.astype(BF16), (((0,), (0,)), ((), ())),
                                     preferred_element_type=F32)
            local.append((q_dec.astype(BF16), intra, update, jnp.exp(last)))
        for ci, (q_in, intra, update, decay) in enumerate(local):
            rows = slice(ci * c, (ci + 1) * c)
            st = state_ref[...]
            o = intra + lax.dot_general(q_in, st.astype(BF16), (((1,), (1,)), ((), ())),
                                        preferred_element_type=F32)
            state_ref[...] = st * decay + update
            o_ref[0, rows, :] = o.astype(o_ref.dtype)

    run_blocks(factored_scores)

    @pl.when(jnp.logical_not(jnp.min(worst) > -GLA_SAFE_LOG_DECAY))
    def _():
        state_ref[...] = state0_ref[...]
        run_blocks(exact_scores)


def _gla(proj, lr, wlr3, blr):
    b, s, _ = proj.shape
    qk_w = proj.shape[2] // 6
    v_w = 2 * qk_w
    dk, dv = qk_w // GLA_HEADS, v_w // GLA_HEADS
    ts = min(GLA_TS, s)
    nh = GLA_HEADS
    return pl.pallas_call(
        _gla_kernel,
        out_shape=jax.ShapeDtypeStruct((b, s, v_w), BF16),
        grid=(b, nh, s // ts),
        in_specs=[pl.BlockSpec((1, ts, dk), lambda bi, h, i: (bi, i, h)),
                  pl.BlockSpec((1, ts, dk), lambda bi, h, i: (bi, i, nh + h)),
                  pl.BlockSpec((1, ts, dv), lambda bi, h, i: (bi, i, nh + h)),
                  pl.BlockSpec((1, ts, LANES), lambda bi, h, i: (bi, i, 0)),
                  pl.BlockSpec((LANES, dk), lambda bi, h, i: (0, h)),
                  pl.BlockSpec((1, dk), lambda bi, h, i: (0, h))],
        out_specs=pl.BlockSpec((1, ts, dv), lambda bi, h, i: (bi, i, h)),
        scratch_shapes=[pltpu.VMEM((dv, dk), F32),
                        pltpu.VMEM((dv, dk), F32),
                        pltpu.VMEM((ts, dk), F32),
                        pltpu.VMEM((GLA_BLOCK, dk), F32)],
        compiler_params=_params("parallel", "parallel", "arbitrary"),
        name="gla",
    )(proj, proj, proj, lr, wlr3, blr)


def _pad_rows(a, rows):
    return jnp.pad(a, ((0, rows - a.shape[0]), (0, 0)))


def kernel(x, positions, ev_norm, ev_w_in, sgu_gain, sgu_w_s, sgu_b, ev_w_out, od_norm, od_w_in,
           gla_w_lr, gla_b_lr, od_w_out, final_norm):
    b, s, d = x.shape
    t = b * s
    x2 = x.reshape(t, d)

    h0 = _rmsnorm(x2, ev_norm[0])
    proj0 = _inproj(h0, ev_w_in, ev_w_in.shape[2]).reshape(b, s, -1)
    dk_ret = proj0.shape[2] // 12 // RET_HEADS
    half = dk_ret // 2
    inv = jnp.power(ROPE_BASE, -jnp.arange(half, dtype=F32) / half).reshape(1, half)
    posf = positions.astype(F32).reshape(b, s, 1)
    ret = _retention(proj0, posf, inv)
    w_out0 = ev_w_out[0].astype(BF16)
    kr = ret.shape[2]
    n_main = od_w_in.shape[2] - GLA_RANK
    w_lr_in = jnp.pad(jnp.tile(od_w_in[0, :, n_main:], (1, 3)), ((0, 0), (0, LANES - 3 * GLA_RANK)))
    x1, h1, lr = _outproj_mid(ret.reshape(t, kr), proj0.reshape(t, -1), x2, w_out0, od_norm[0],
                              w_lr_in.astype(BF16), sgu_gain[0], sgu_w_s[0], sgu_b[0])

    proj1 = _inproj(h1, jnp.swapaxes(od_w_in, 1, 2), n_main, w_transposed=True)
    w_hi, w_lo = _split_bf16(gla_w_lr[0])
    wlr3 = _pad_rows(jnp.concatenate([w_hi, w_hi, w_lo], axis=0), LANES)
    attn = _gla(proj1.reshape(b, s, -1), lr.reshape(b, s, LANES), wlr3, gla_b_lr[0].reshape(1, -1))
    dk_gla = n_main // 6 // GLA_HEADS
    out = _outproj_final(attn.reshape(t, -1), proj1, x1, od_w_out[0].astype(BF16), final_norm,
                         dv=2 * dk_gla, head_eps=EPS * dk_gla)
    return out.reshape(b, s, d)
```

```python
import functools

import numpy as np
import jax
import jax.numpy as jnp
from jax import lax
from jax.experimental import pallas as pl
from jax.experimental.pallas import tpu as pltpu

F32 = jnp.float32
BF16 = jnp.bfloat16

EPS = 1e-6
RET_HEADS = 4
RET_BLOCK = 256
ROPE_BASE = 10000.0
SGU_GROUPS = 8
SGU_CHUNK = 128
GLA_HEADS = 8
GLA_RANK = 16
GLA_TAU = 16.0
GLA_BLOCK = 128
GLA_SAFE_LOG_DECAY = 60.0

LANES = 128
VMEM_LIMIT = 56 * 1024 * 1024

NORM_TM = 512
PROJ_TM = 1024
PROJ_TN = 1024
OUT_TM = 256
RET_TS = 512
MID_TM = 256
GLA_TS = 1024


def _params(*sem):
    return pltpu.CompilerParams(dimension_semantics=sem, vmem_limit_bytes=VMEM_LIMIT)


def _silu(g):
    return g * (1.0 / (1.0 + jnp.exp(-g)))


def _rms(x, gain):
    return x * lax.rsqrt(jnp.mean(x * x, axis=-1, keepdims=True) + EPS) * gain


def _rmsnorm_kernel(x_ref, g_ref, o_ref):
    o_ref[...] = _rms(x_ref[...], g_ref[...]).astype(o_ref.dtype)


def _rmsnorm(x, gain):
    t, d = x.shape
    tm = min(NORM_TM, t)
    return pl.pallas_call(
        _rmsnorm_kernel,
        out_shape=jax.ShapeDtypeStruct((t, d), BF16),
        grid=(t // tm,),
        in_specs=[pl.BlockSpec((tm, d), lambda i: (i, 0)),
                  pl.BlockSpec((1, d), lambda i: (0, 0))],
        out_specs=pl.BlockSpec((tm, d), lambda i: (i, 0)),
        compiler_params=_params("parallel"),
        name="rmsnorm",
    )(x, gain.reshape(1, d))


def _inproj_kernel(h_ref, w_ref, o_ref, wbf_ref, *, w_transposed):
    @pl.when(pl.program_id(1) == 0)
    def _():
        wbf_ref[...] = w_ref[...].astype(BF16)

    contract_w = 1 if w_transposed else 0
    o_ref[...] = lax.dot_general(h_ref[...], wbf_ref[...], (((1,), (contract_w,)), ((), ())),
                                 preferred_element_type=F32).astype(o_ref.dtype)


def _inproj(h, w, n, w_transposed=False):
    t, d = h.shape
    tm, tn = min(PROJ_TM, t), min(PROJ_TN, n)
    if w_transposed:
        w_spec = pl.BlockSpec((None, tn, d), lambda j, i: (0, j, 0))
    else:
        w_spec = pl.BlockSpec((None, d, tn), lambda j, i: (0, 0, j))
    return pl.pallas_call(
        functools.partial(_inproj_kernel, w_transposed=w_transposed),
        out_shape=jax.ShapeDtypeStruct((t, n), BF16),
        grid=(n // tn, t // tm),
        in_specs=[pl.BlockSpec((tm, d), lambda j, i: (i, 0)), w_spec],
        out_specs=pl.BlockSpec((tm, tn), lambda j, i: (i, j)),
        scratch_shapes=[pltpu.VMEM((tn, d) if w_transposed else (d, tn), BF16)],
        compiler_params=_params("parallel", "arbitrary"),
        name="inproj",
    )(h, w)


def _retention_tables(dk):
    c = RET_BLOCK
    h = np.arange(RET_HEADS, dtype=np.float64)
    log_gamma = np.log1p(-np.power(2.0, -5.0 - h))
    idx = np.arange(c, dtype=np.float64)
    rel = idx[:, None] - idx[None, :]
    dmat = np.where(rel >= 0, np.exp(log_gamma[:, None, None] * np.maximum(rel, 0.0)), 0.0)
    qdec = np.exp(log_gamma[:, None] * (idx + 1.0))
    kdec = np.exp(log_gamma[:, None] * (c - 1.0 - idx))
    cdec = np.exp(log_gamma * c)
    lanes = lambda a: np.broadcast_to(a[:, :, None], (RET_HEADS, c, LANES))
    return (jnp.asarray(dmat, F32), jnp.asarray(lanes(qdec), F32), jnp.asarray(lanes(kdec), F32),
            [float(v) for v in cdec], float(dk) ** -0.5)


def _retention_kernel(q_ref, k_ref, v_ref, pos_ref, inv_ref, dmat_ref, qdec_ref, kdec_ref,
                      o_ref, state_ref, *, cdec, scale, dk, dv):
    c = RET_BLOCK
    half = dk // 2
    ts = q_ref.shape[1]

    @pl.when(pl.program_id(1) == 0)
    def _():
        state_ref[...] = jnp.zeros_like(state_ref)

    def chunk(ci, carry):
        rows = pl.ds(pl.multiple_of(ci * c, c), c)
        ang = pos_ref[0, rows, :] * inv_ref[...]
        cos, sin = jnp.cos(ang), jnp.sin(ang)
        cos_q, sin_q = cos * scale, sin * scale
        for h in range(RET_HEADS):
            q = q_ref[0, rows, h * dk:(h + 1) * dk].astype(F32)
            k = k_ref[0, rows, h * dk:(h + 1) * dk].astype(F32)
            q1, q2 = q[:, :half], q[:, half:]
            k1, k2 = k[:, :half], k[:, half:]
            qr1, qr2 = q1 * cos_q - q2 * sin_q, q1 * sin_q + q2 * cos_q
            kr1, kr2 = k1 * cos - k2 * sin, k1 * sin + k2 * cos
            qdec, kdec = qdec_ref[h], kdec_ref[h]
            qr = jnp.concatenate([qr1, qr2], axis=-1).astype(BF16)
            kr = jnp.concatenate([kr1, kr2], axis=-1).astype(BF16)
            qd = jnp.concatenate([qr1 * qdec, qr2 * qdec], axis=-1).astype(BF16)
            kd = jnp.concatenate([kr1 * kdec, kr2 * kdec], axis=-1).astype(BF16)
            s = lax.dot_general(qr, kr, (((1,), (1,)), ((), ())), preferred_element_type=F32)
            p = (s * dmat_ref[h]).astype(BF16)
            v = v_ref[0, rows, h * dv:(h + 1) * dv]
            st = state_ref[h]
            o = (jnp.dot(p, v, preferred_element_type=F32)
                 + jnp.dot(qd, st.astype(BF16), preferred_element_type=F32))
            state_ref[h] = cdec[h] * st + lax.dot_general(
                kd, v, (((0,), (0,)), ((), ())), preferred_element_type=F32)
            o_ref[0, rows, h * dv:(h + 1) * dv] = o.astype(o_ref.dtype)
        return carry

    lax.fori_loop(0, ts // c, chunk, 0)


def _retention(proj, posf, inv):
    b, s, _ = proj.shape
    qk_w = proj.shape[2] // 12
    v_w = 2 * qk_w
    dk, dv = qk_w // RET_HEADS, v_w // RET_HEADS
    ts = min(RET_TS, s)
    dmat, qdec, kdec, cdec, scale = _retention_tables(dk)
    const3 = lambda shape: pl.BlockSpec(shape, lambda bi, i: (0, 0, 0))
    return pl.pallas_call(
        functools.partial(_retention_kernel, cdec=cdec, scale=scale, dk=dk, dv=dv),
        out_shape=jax.ShapeDtypeStruct((b, s, v_w), BF16),
        grid=(b, s // ts),
        in_specs=[pl.BlockSpec((1, ts, qk_w), lambda bi, i: (bi, i, 0)),
                  pl.BlockSpec((1, ts, qk_w), lambda bi, i: (bi, i, 1)),
                  pl.BlockSpec((1, ts, v_w), lambda bi, i: (bi, i, 1)),
                  pl.BlockSpec((1, ts, 1), lambda bi, i: (bi, i, 0)),
                  pl.BlockSpec((1, dk // 2), lambda bi, i: (0, 0)),
                  const3(dmat.shape), const3(qdec.shape), const3(kdec.shape)],
        out_specs=pl.BlockSpec((1, ts, v_w), lambda bi, i: (bi, i, 0)),
        scratch_shapes=[pltpu.VMEM((RET_HEADS, dk, dv), F32)],
        compiler_params=_params("parallel", "arbitrary"),
        name="retention",
    )(proj, proj, proj, posf, inv, dmat, qdec, kdec)


def _sgu_block(u_ref, v_ref, g_ref, gain_ref, ws_ref, sb_ref, sgu_ref, *, dg):
    c = SGU_CHUNK
    row = lax.broadcasted_iota(jnp.int32, (c, c), 0)
    col = lax.broadcasted_iota(jnp.int32, (c, c), 1)
    causal = row >= col
    wm = [jnp.where(causal, ws_ref[g], 0.0).astype(BF16) for g in range(SGU_GROUPS)]
    for ci in range(u_ref.shape[0] // c):
        rows = slice(ci * c, (ci + 1) * c)
        for g in range(SGU_GROUPS):
            cols = slice(g * dg, (g + 1) * dg)
            v = v_ref[rows, cols].astype(F32)
            d = v - jnp.mean(v, axis=-1, keepdims=True)
            var = jnp.mean(d * d, axis=-1, keepdims=True)
            vn = d * lax.rsqrt(var + EPS) * gain_ref[:, cols]
            sg = jnp.dot(wm[g], vn.astype(BF16), preferred_element_type=F32) + sb_ref[g]
            sgu_ref[rows, cols] = sg.astype(BF16) * (u_ref[rows, cols] * _silu(g_ref[rows, cols]))


def _gated_heads_matmul(acc, a_ref, g_ref, w_ref, dv, head_eps):
    for h in range(a_ref.shape[1] // dv):
        cols = slice(h * dv, (h + 1) * dv)
        o = a_ref[:, cols].astype(F32)
        on = o * lax.rsqrt(jnp.mean(o * o, axis=-1, keepdims=True) + head_eps)
        mixed = on.astype(BF16) * _silu(g_ref[:, cols])
        acc = acc + jnp.dot(mixed, w_ref[cols, :], preferred_element_type=F32)
    return acc


def _outproj_mid_kernel(a_ref, ga_ref, u_ref, v_ref, g_ref, x_ref, wa_ref, wb_ref, gn_ref, wlr_ref,
                        sgain_ref, ws_ref, sb_ref, x1_ref, h1_ref, lr_ref, sgu_ref, *, dg, dv):
    acc = _gated_heads_matmul(x_ref[...], a_ref, ga_ref, wa_ref, dv, EPS)
    _sgu_block(u_ref, v_ref, g_ref, sgain_ref, ws_ref, sb_ref, sgu_ref, dg=dg)
    x1 = acc + jnp.dot(sgu_ref[...], wb_ref[...], preferred_element_type=F32)
    x1_ref[...] = x1
    h1 = _rms(x1, gn_ref[...]).astype(h1_ref.dtype)
    h1_ref[...] = h1
    lr_ref[...] = jnp.dot(h1, wlr_ref[...], preferred_element_type=F32)


def _outproj_final_kernel(a_ref, g_ref, x_ref, w_ref, gn_ref, o_ref, *, dv, head_eps):
    x1 = _gated_heads_matmul(x_ref[...], a_ref, g_ref, w_ref, dv, head_eps)
    o_ref[...] = _rms(x1, gn_ref[...])


def _resident(shape):
    return pl.BlockSpec(shape, lambda i: (0, 0), pipeline_mode=pl.Buffered(1))


def _outproj_mid(a, proj, x, w, gain, wlr, sgu_gain, w_s, b_s, dv):
    t, d = x.shape
    ka = a.shape[1]
    kb = proj.shape[1] // 6
    assert ka == kb and w.shape == (ka + kb, d)
    dg = kb // SGU_GROUPS
    tm = min(MID_TM, t)
    bias = jnp.broadcast_to(b_s[:, :, None], (SGU_GROUPS, SGU_CHUNK, dg))
    row = lambda w_, blk=0: pl.BlockSpec((tm, w_), lambda i: (i, blk))
    w_half = lambda blk: pl.BlockSpec((ka, d), lambda i: (blk, 0), pipeline_mode=pl.Buffered(1))
    const3 = lambda shape: pl.BlockSpec(shape, lambda i: (0, 0, 0), pipeline_mode=pl.Buffered(1))
    return pl.pallas_call(
        functools.partial(_outproj_mid_kernel, dg=dg, dv=dv),
        out_shape=(jax.ShapeDtypeStruct((t, d), F32), jax.ShapeDtypeStruct((t, d), BF16),
                   jax.ShapeDtypeStruct((t, LANES), F32)),
        grid=(t // tm,),
        in_specs=[row(ka), row(ka, 2), row(kb, 3), row(kb, 4), row(kb, 5), row(d), w_half(0), w_half(1),
                  _resident((1, d)), _resident((d, LANES)), _resident((1, kb)),
                  const3(w_s.shape), const3(bias.shape)],
        out_specs=(row(d), row(d), row(LANES)),
        scratch_shapes=[pltpu.VMEM((tm, kb), BF16)],
        compiler_params=_params("parallel"),
        name="outproj_mid",
    )(a, proj, proj, proj, proj, x, w, w, gain.reshape(1, d), wlr, sgu_gain.reshape(1, kb), w_s, bias)


def _outproj_final(a, proj, x, w, gain, dv, head_eps):
    t, d = x.shape
    ka = a.shape[1]
    tm = min(OUT_TM, t)
    row = lambda w_, blk=0: pl.BlockSpec((tm, w_), lambda i: (i, blk))
    return pl.pallas_call(
        functools.partial(_outproj_final_kernel, dv=dv, head_eps=head_eps),
        out_shape=jax.ShapeDtypeStruct((t, d), F32),
        grid=(t // tm,),
        in_specs=[row(ka), row(ka, 2), row(d), _resident((ka, d)), _resident((1, d))],
        out_specs=row(d),
        compiler_params=_params("parallel"),
        name="outproj_final",
    )(a, proj, x, w, gain.reshape(1, d))


def _split_bf16(x):
    hi = x.astype(BF16)
    lo = (x - hi.astype(F32)).astype(BF16)
    return hi, lo


def _gla_kernel(q_ref, k_ref, v_ref, lr_ref, wlr_ref, blr_ref, o_ref,
                state_ref, state0_ref, bc_ref, kf_ref):
    c = GLA_BLOCK
    ts = q_ref.shape[1]
    n_blocks = ts // c

    @pl.when(pl.program_id(2) == 0)
    def _():
        state_ref[...] = jnp.zeros_like(state_ref)

    state0_ref[...] = state_ref[...]

    lr_hi, lr_lo = _split_bf16(lr_ref[0])
    lane = lax.broadcasted_iota(jnp.int32, lr_hi.shape, 1)
    lr3 = jnp.where((lane >= GLA_RANK) & (lane < 2 * GLA_RANK), lr_lo, lr_hi)
    z = jnp.dot(lr3, wlr_ref[...], preferred_element_type=F32) + blr_ref[...]
    log_alpha = (jnp.minimum(z, 0.0) - jnp.log(1.0 + jnp.exp(-jnp.abs(z)))) * (1.0 / GLA_TAU)

    row = lax.broadcasted_iota(jnp.int32, (c, c), 0)
    col = lax.broadcasted_iota(jnp.int32, (c, c), 1)
    causal = row >= col
    tri = jnp.where(causal, 1.0, 0.0).astype(BF16)
    tri2 = jnp.concatenate([tri, tri], axis=1)

    worst = None
    for ci in range(n_blocks):
        rows = slice(ci * c, (ci + 1) * c)
        la_hi, la_lo = _split_bf16(log_alpha[rows])
        bc = jnp.dot(tri2, jnp.concatenate([la_hi, la_lo], axis=0), preferred_element_type=F32)
        bc_ref[rows, :] = bc
        worst = bc[c - 1:c, :] if worst is None else jnp.minimum(worst, bc[c - 1:c, :])

    def factored_scores(ci, bc, q, k, q_dec, k_dec):
        mid = bc[c // 2 - 1:c // 2, :]
        last = bc[c - 1:c, :]
        q_t = (q_dec * jnp.exp(-mid)).astype(BF16)
        k_t = (k_dec * jnp.exp(mid - last)).astype(BF16)
        s = lax.dot_general(q_t, k_t, (((1,), (1,)), ((), ())), preferred_element_type=F32)
        return jnp.where(causal, s, 0.0).astype(BF16)

    def exact_scores(ci, bc, q, k, q_dec, k_dec):
        kf_ref[...] = k

        def column(s, p):
            w = jnp.exp(jnp.minimum(bc - bc_ref[pl.ds(ci * c + s, 1), :], 0.0))
            val = jnp.sum(q * w * kf_ref[pl.ds(s, 1), :], axis=-1, keepdims=True)
            return jnp.where(col == s, val, p)

        p = lax.fori_loop(0, c, column, jnp.zeros((c, c), F32))
        return jnp.where(causal, p, 0.0).astype(BF16)

    def run_blocks(scores_fn):
        local = []
        for ci in range(n_blocks):
            rows = slice(ci * c, (ci + 1) * c)
            bc = bc_ref[rows, :]
            last = bc[c - 1:c, :]
            q = q_ref[0, rows, :].astype(F32)
            k = k_ref[0, rows, :].astype(F32)
            v = v_ref[0, rows, :]
            q_dec = q * jnp.exp(bc)
            k_dec = k * jnp.exp(last - bc)
            p = scores_fn(ci, bc, q, k, q_dec, k_dec)
            intra = jnp.dot(p, v, preferred_element_type=F32)
            update = lax.dot_general(v, k_dec.astype(BF16), (((0,), (0,)), ((), ())),
                                     preferred_element_type=F32)
            local.append((q_dec.astype(BF16), intra, update, jnp.exp(last)))
        for ci, (q_in, intra, update, decay) in enumerate(local):
            rows = slice(ci * c, (ci + 1) * c)
            st = state_ref[...]
            o = intra + lax.dot_general(q_in, st.astype(BF16), (((1,), (1,)), ((), ())),
                                        preferred_element_type=F32)
            state_ref[...] = st * decay + update
            o_ref[0, rows, :] = o.astype(o_ref.dtype)

    run_blocks(factored_scores)

    @pl.when(jnp.logical_not(jnp.min(worst) > -GLA_SAFE_LOG_DECAY))
    def _():
        state_ref[...] = state0_ref[...]
        run_blocks(exact_scores)


def _gla(proj, lr, wlr3, blr):
    b, s, _ = proj.shape
    qk_w = proj.shape[2] // 6
    v_w = 2 * qk_w
    dk, dv = qk_w // GLA_HEADS, v_w // GLA_HEADS
    ts = min(GLA_TS, s)
    nh = GLA_HEADS
    return pl.pallas_call(
        _gla_kernel,
        out_shape=jax.ShapeDtypeStruct((b, s, v_w), BF16),
        grid=(b, nh, s // ts),
        in_specs=[pl.BlockSpec((1, ts, dk), lambda bi, h, i: (bi, i, h)),
                  pl.BlockSpec((1, ts, dk), lambda bi, h, i: (bi, i, nh + h)),
                  pl.BlockSpec((1, ts, dv), lambda bi, h, i: (bi, i, nh + h)),
                  pl.BlockSpec((1, ts, LANES), lambda bi, h, i: (bi, i, 0)),
                  pl.BlockSpec((LANES, dk), lambda bi, h, i: (0, h)),
                  pl.BlockSpec((1, dk), lambda bi, h, i: (0, h))],
        out_specs=pl.BlockSpec((1, ts, dv), lambda bi, h, i: (bi, i, h)),
        scratch_shapes=[pltpu.VMEM((dv, dk), F32),
                        pltpu.VMEM((dv, dk), F32),
                        pltpu.VMEM((ts, dk), F32),
                        pltpu.VMEM((GLA_BLOCK, dk), F32)],
        compiler_params=_params("parallel", "parallel", "arbitrary"),
        name="gla",
    )(proj, proj, proj, lr, wlr3, blr)


def _pad_rows(a, rows):
    return jnp.pad(a, ((0, rows - a.shape[0]), (0, 0)))


def kernel(x, positions, ev_norm, ev_w_in, sgu_gain, sgu_w_s, sgu_b, ev_w_out, od_norm, od_w_in,
           gla_w_lr, gla_b_lr, od_w_out, final_norm):
    b, s, d = x.shape
    t = b * s
    x2 = x.reshape(t, d)

    h0 = _rmsnorm(x2, ev_norm[0])
    proj0 = _inproj(h0, ev_w_in, ev_w_in.shape[2]).reshape(b, s, -1)
    dk_ret = proj0.shape[2] // 12 // RET_HEADS
    half = dk_ret // 2
    inv = jnp.power(ROPE_BASE, -jnp.arange(half, dtype=F32) / half).reshape(1, half)
    posf = positions.astype(F32).reshape(b, s, 1)
    ret = _retention(proj0, posf, inv)
    w_out0 = ev_w_out[0].astype(BF16)
    kr = ret.shape[2]
    n_main = od_w_in.shape[2] - GLA_RANK
    w_lr_in = jnp.pad(jnp.tile(od_w_in[0, :, n_main:], (1, 3)), ((0, 0), (0, LANES - 3 * GLA_RANK)))
    x1, h1, lr = _outproj_mid(ret.reshape(t, kr), proj0.reshape(t, -1), x2, w_out0, od_norm[0],
                              w_lr_in.astype(BF16), sgu_gain[0], sgu_w_s[0], sgu_b[0],
                              dv=kr // RET_HEADS)

    proj1 = _inproj(h1, jnp.swapaxes(od_w_in, 1, 2), n_main, w_transposed=True)
    w_hi, w_lo = _split_bf16(gla_w_lr[0])
    wlr3 = _pad_rows(jnp.concatenate([w_hi, w_hi, w_lo], axis=0), LANES)
    attn = _gla(proj1.reshape(b, s, -1), lr.reshape(b, s, LANES), wlr3, gla_b_lr[0].reshape(1, -1))
    dk_gla = n_main // 6 // GLA_HEADS
    out = _outproj_final(attn.reshape(t, -1), proj1, x1, od_w_out[0].astype(BF16), final_norm,
                         dv=2 * dk_gla, head_eps=EPS * dk_gla)
    return out.reshape(b, s, d)
```

```python
import functools

import numpy as np
import jax
import jax.numpy as jnp
from jax import lax
from jax.experimental import pallas as pl
from jax.experimental.pallas import tpu as pltpu

F32 = jnp.float32
BF16 = jnp.bfloat16

EPS = 1e-6
RET_HEADS = 4
RET_BLOCK = 256
ROPE_BASE = 10000.0
SGU_GROUPS = 8
SGU_CHUNK = 128
GLA_HEADS = 8
GLA_RANK = 16
GLA_TAU = 16.0
GLA_BLOCK = 128
GLA_SAFE_LOG_DECAY = 60.0

LANES = 128
VMEM_LIMIT = 56 * 1024 * 1024

NORM_TM = 512
PROJ_TM = 1024
PROJ_TN = 1024
OUT_TM = 256
RET_TS = 1024
MID_TM = 256
GLA_TS = 1024


def _params(*sem):
    return pltpu.CompilerParams(dimension_semantics=sem, vmem_limit_bytes=VMEM_LIMIT)


def _silu(g):
    return g * (1.0 / (1.0 + jnp.exp(-g)))


def _rms(x, gain):
    return x * lax.rsqrt(jnp.mean(x * x, axis=-1, keepdims=True) + EPS) * gain


def _rmsnorm_kernel(x_ref, g_ref, o_ref):
    o_ref[...] = _rms(x_ref[...], g_ref[...]).astype(o_ref.dtype)


def _rmsnorm(x, gain):
    t, d = x.shape
    tm = min(NORM_TM, t)
    return pl.pallas_call(
        _rmsnorm_kernel,
        out_shape=jax.ShapeDtypeStruct((t, d), BF16),
        grid=(t // tm,),
        in_specs=[pl.BlockSpec((tm, d), lambda i: (i, 0)),
                  pl.BlockSpec((1, d), lambda i: (0, 0))],
        out_specs=pl.BlockSpec((tm, d), lambda i: (i, 0)),
        compiler_params=_params("parallel"),
        name="rmsnorm",
    )(x, gain.reshape(1, d))


def _inproj_kernel(h_ref, w_ref, o_ref, wbf_ref, *, w_transposed):
    @pl.when(pl.program_id(1) == 0)
    def _():
        wbf_ref[...] = w_ref[...].astype(BF16)

    contract_w = 1 if w_transposed else 0
    o_ref[...] = lax.dot_general(h_ref[...], wbf_ref[...], (((1,), (contract_w,)), ((), ())),
                                 preferred_element_type=F32).astype(o_ref.dtype)


def _inproj(h, w, n, w_transposed=False):
    t, d = h.shape
    tm, tn = min(PROJ_TM, t), min(PROJ_TN, n)
    if w_transposed:
        w_spec = pl.BlockSpec((None, tn, d), lambda j, i: (0, j, 0))
    else:
        w_spec = pl.BlockSpec((None, d, tn), lambda j, i: (0, 0, j))
    return pl.pallas_call(
        functools.partial(_inproj_kernel, w_transposed=w_transposed),
        out_shape=jax.ShapeDtypeStruct((t, n), BF16),
        grid=(n // tn, t // tm),
        in_specs=[pl.BlockSpec((tm, d), lambda j, i: (i, 0)), w_spec],
        out_specs=pl.BlockSpec((tm, tn), lambda j, i: (i, j)),
        scratch_shapes=[pltpu.VMEM((tn, d) if w_transposed else (d, tn), BF16)],
        compiler_params=_params("parallel", "arbitrary"),
        name="inproj",
    )(h, w)


def _retention_tables(dk):
    c = RET_BLOCK
    h = np.arange(RET_HEADS, dtype=np.float64)
    log_gamma = np.log1p(-np.power(2.0, -5.0 - h))
    idx = np.arange(c, dtype=np.float64)
    rel = idx[:, None] - idx[None, :]
    dmat = np.where(rel >= 0, np.exp(log_gamma[:, None, None] * np.maximum(rel, 0.0)), 0.0)
    qdec = np.exp(log_gamma[:, None] * (idx + 1.0))
    kdec = np.exp(log_gamma[:, None] * (c - 1.0 - idx))
    cdec = np.exp(log_gamma * c)
    lanes = lambda a: np.broadcast_to(a[:, :, None], (RET_HEADS, c, dk // 2))
    return (jnp.asarray(dmat, F32), jnp.asarray(lanes(qdec), F32), jnp.asarray(lanes(kdec), F32),
            [float(v) for v in cdec], float(dk) ** -0.5)


def _retention_kernel(q_ref, k_ref, v_ref, pos_ref, inv_ref, dmat_ref, qdec_ref, kdec_ref,
                      o_ref, state_ref, *, cdec, scale, dk, dv):
    c = RET_BLOCK
    half = dk // 2
    ts = q_ref.shape[1]

    @pl.when(pl.program_id(1) == 0)
    def _():
        state_ref[...] = jnp.zeros_like(state_ref)

    def chunk(ci, carry):
        rows = pl.ds(pl.multiple_of(ci * c, c), c)
        ang = pos_ref[0, rows, :] * inv_ref[...]
        cos, sin = jnp.cos(ang), jnp.sin(ang)
        cos_q, sin_q = cos * scale, sin * scale
        for h in range(RET_HEADS):
            q = q_ref[0, rows, h * dk:(h + 1) * dk].astype(F32)
            k = k_ref[0, rows, h * dk:(h + 1) * dk].astype(F32)
            q1, q2 = q[:, :half], q[:, half:]
            k1, k2 = k[:, :half], k[:, half:]
            qr1, qr2 = q1 * cos_q - q2 * sin_q, q1 * sin_q + q2 * cos_q
            kr1, kr2 = k1 * cos - k2 * sin, k1 * sin + k2 * cos
            qdec, kdec = qdec_ref[h], kdec_ref[h]
            qr = jnp.concatenate([qr1, qr2], axis=-1).astype(BF16)
            kr = jnp.concatenate([kr1, kr2], axis=-1).astype(BF16)
            qd = jnp.concatenate([qr1 * qdec, qr2 * qdec], axis=-1).astype(BF16)
            kd = jnp.concatenate([kr1 * kdec, kr2 * kdec], axis=-1).astype(BF16)
            s = lax.dot_general(qr, kr, (((1,), (1,)), ((), ())), preferred_element_type=F32)
            p = (s * dmat_ref[h]).astype(BF16)
            v = v_ref[0, rows, h * dv:(h + 1) * dv]
            st = state_ref[h]
            o = (jnp.dot(p, v, preferred_element_type=F32)
                 + jnp.dot(qd, st.astype(BF16), preferred_element_type=F32))
            state_ref[h] = cdec[h] * st + lax.dot_general(
                kd, v, (((0,), (0,)), ((), ())), preferred_element_type=F32)
            o_ref[0, rows, h * dv:(h + 1) * dv] = o.astype(o_ref.dtype)
        return carry

    lax.fori_loop(0, ts // c, chunk, 0)


def _retention(proj, posf, inv):
    b, s, _ = proj.shape
    qk_w = proj.shape[2] // 12
    v_w = 2 * qk_w
    dk, dv = qk_w // RET_HEADS, v_w // RET_HEADS
    ts = min(RET_TS, s)
    dmat, qdec, kdec, cdec, scale = _retention_tables(dk)
    const3 = lambda shape: pl.BlockSpec(shape, lambda bi, i: (0, 0, 0))
    return pl.pallas_call(
        functools.partial(_retention_kernel, cdec=cdec, scale=scale, dk=dk, dv=dv),
        out_shape=jax.ShapeDtypeStruct((b, s, v_w), BF16),
        grid=(b, s // ts),
        in_specs=[pl.BlockSpec((1, ts, qk_w), lambda bi, i: (bi, i, 0)),
                  pl.BlockSpec((1, ts, qk_w), lambda bi, i: (bi, i, 1)),
                  pl.BlockSpec((1, ts, v_w), lambda bi, i: (bi, i, 1)),
                  pl.BlockSpec((1, ts, 1), lambda bi, i: (bi, i, 0)),
                  pl.BlockSpec((1, dk // 2), lambda bi, i: (0, 0)),
                  const3(dmat.shape), const3(qdec.shape), const3(kdec.shape)],
        out_specs=pl.BlockSpec((1, ts, v_w), lambda bi, i: (bi, i, 0)),
        scratch_shapes=[pltpu.VMEM((RET_HEADS, dk, dv), F32)],
        compiler_params=_params("parallel", "arbitrary"),
        name="retention",
    )(proj, proj, proj, posf, inv, dmat, qdec, kdec)


def _sgu_block(u_ref, v_ref, g_ref, gain_ref, ws_ref, sb_ref, sgu_ref, *, dg):
    c = SGU_CHUNK
    row = lax.broadcasted_iota(jnp.int32, (c, c), 0)
    col = lax.broadcasted_iota(jnp.int32, (c, c), 1)
    causal = row >= col
    wm = [jnp.where(causal, ws_ref[g], 0.0).astype(BF16) for g in range(SGU_GROUPS)]
    for ci in range(u_ref.shape[0] // c):
        rows = slice(ci * c, (ci + 1) * c)
        for g in range(SGU_GROUPS):
            cols = slice(g * dg, (g + 1) * dg)
            v = v_ref[rows, cols].astype(F32)
            d = v - jnp.mean(v, axis=-1, keepdims=True)
            var = jnp.mean(d * d, axis=-1, keepdims=True)
            vn = d * lax.rsqrt(var + EPS) * gain_ref[:, cols]
            sg = jnp.dot(wm[g], vn.astype(BF16), preferred_element_type=F32) + sb_ref[g]
            sgu_ref[rows, cols] = sg.astype(BF16) * (u_ref[rows, cols] * _silu(g_ref[rows, cols]))


def _gated_heads_matmul(acc, a_ref, g_ref, w_ref, dv, head_eps):
    for h in range(a_ref.shape[1] // dv):
        cols = slice(h * dv, (h + 1) * dv)
        o = a_ref[:, cols].astype(F32)
        on = o * lax.rsqrt(jnp.mean(o * o, axis=-1, keepdims=True) + head_eps)
        mixed = on.astype(BF16) * _silu(g_ref[:, cols])
        acc = acc + jnp.dot(mixed, w_ref[cols, :], preferred_element_type=F32)
    return acc


def _outproj_mid_kernel(a_ref, ga_ref, u_ref, v_ref, g_ref, x_ref, wa_ref, wb_ref, gn_ref, wlr_ref,
                        sgain_ref, ws_ref, sb_ref, x1_ref, h1_ref, lr_ref, sgu_ref, *, dg, dv):
    acc = _gated_heads_matmul(x_ref[...], a_ref, ga_ref, wa_ref, dv, EPS)
    _sgu_block(u_ref, v_ref, g_ref, sgain_ref, ws_ref, sb_ref, sgu_ref, dg=dg)
    x1 = acc + jnp.dot(sgu_ref[...], wb_ref[...], preferred_element_type=F32)
    x1_ref[...] = x1
    h1 = _rms(x1, gn_ref[...]).astype(h1_ref.dtype)
    h1_ref[...] = h1
    lr_ref[...] = jnp.dot(h1, wlr_ref[...], preferred_element_type=F32)


def _outproj_final_kernel(a_ref, g_ref, x_ref, w_ref, gn_ref, o_ref, *, dv, head_eps):
    x1 = _gated_heads_matmul(x_ref[...], a_ref, g_ref, w_ref, dv, head_eps)
    o_ref[...] = _rms(x1, gn_ref[...])


def _resident(shape):
    return pl.BlockSpec(shape, lambda i: (0, 0), pipeline_mode=pl.Buffered(1))


def _outproj_mid(a, proj, x, w, gain, wlr, sgu_gain, w_s, b_s, dv):
    t, d = x.shape
    ka = a.shape[1]
    kb = proj.shape[1] // 6
    assert ka == kb and w.shape == (ka + kb, d)
    dg = kb // SGU_GROUPS
    tm = min(MID_TM, t)
    bias = jnp.broadcast_to(b_s[:, :, None], (SGU_GROUPS, SGU_CHUNK, dg))
    row = lambda w_, blk=0: pl.BlockSpec((tm, w_), lambda i: (i, blk))
    w_half = lambda blk: pl.BlockSpec((ka, d), lambda i: (blk, 0), pipeline_mode=pl.Buffered(1))
    const3 = lambda shape: pl.BlockSpec(shape, lambda i: (0, 0, 0), pipeline_mode=pl.Buffered(1))
    return pl.pallas_call(
        functools.partial(_outproj_mid_kernel, dg=dg, dv=dv),
        out_shape=(jax.ShapeDtypeStruct((t, d), F32), jax.ShapeDtypeStruct((t, d), BF16),
                   jax.ShapeDtypeStruct((t, LANES), F32)),
        grid=(t // tm,),
        in_specs=[row(ka), row(ka, 2), row(kb, 3), row(kb, 4), row(kb, 5), row(d), w_half(0), w_half(1),
                  _resident((1, d)), _resident((d, LANES)), _resident((1, kb)),
                  const3(w_s.shape), const3(bias.shape)],
        out_specs=(row(d), row(d), row(LANES)),
        scratch_shapes=[pltpu.VMEM((tm, kb), BF16)],
        compiler_params=_params("parallel"),
        name="outproj_mid",
    )(a, proj, proj, proj, proj, x, w, w, gain.reshape(1, d), wlr, sgu_gain.reshape(1, kb), w_s, bias)


def _outproj_final(a, proj, x, w, gain, dv, head_eps):
    t, d = x.shape
    ka = a.shape[1]
    tm = min(OUT_TM, t)
    row = lambda w_, blk=0: pl.BlockSpec((tm, w_), lambda i: (i, blk))
    return pl.pallas_call(
        functools.partial(_outproj_final_kernel, dv=dv, head_eps=head_eps),
        out_shape=jax.ShapeDtypeStruct((t, d), F32),
        grid=(t // tm,),
        in_specs=[row(ka), row(ka, 2), row(d), _resident((ka, d)), _resident((1, d))],
        out_specs=row(d),
        compiler_params=_params("parallel"),
        name="outproj_final",
    )(a, proj, x, w, gain.reshape(1, d))


def _split_bf16(x):
    hi = x.astype(BF16)
    lo = (x - hi.astype(F32)).astype(BF16)
    return hi, lo


def _gla_kernel(q_ref, k_ref, v_ref, lr_ref, wlr_ref, blr_ref, o_ref,
                state_ref, state0_ref, bc_ref, kf_ref):
    c = GLA_BLOCK
    ts = q_ref.shape[1]
    n_blocks = ts // c

    @pl.when(pl.program_id(2) == 0)
    def _():
        state_ref[...] = jnp.zeros_like(state_ref)

    state0_ref[...] = state_ref[...]

    lr_hi, lr_lo = _split_bf16(lr_ref[0])
    lane = lax.broadcasted_iota(jnp.int32, lr_hi.shape, 1)
    lr3 = jnp.where((lane >= GLA_RANK) & (lane < 2 * GLA_RANK), lr_lo, lr_hi)
    z = jnp.dot(lr3, wlr_ref[...], preferred_element_type=F32) + blr_ref[...]
    log_alpha = (jnp.minimum(z, 0.0) - jnp.log(1.0 + jnp.exp(-jnp.abs(z)))) * (1.0 / GLA_TAU)

    row = lax.broadcasted_iota(jnp.int32, (c, c), 0)
    col = lax.broadcasted_iota(jnp.int32, (c, c), 1)
    causal = row >= col
    tri = jnp.where(causal, 1.0, 0.0).astype(BF16)
    tri2 = jnp.concatenate([tri, tri], axis=1)

    worst = None
    for ci in range(n_blocks):
        rows = slice(ci * c, (ci + 1) * c)
        la_hi, la_lo = _split_bf16(log_alpha[rows])
        bc = jnp.dot(tri2, jnp.concatenate([la_hi, la_lo], axis=0), preferred_element_type=F32)
        bc_ref[rows, :] = bc
        worst = bc[c - 1:c, :] if worst is None else jnp.minimum(worst, bc[c - 1:c, :])

    def factored_scores(ci, bc, q, k, q_dec, k_dec):
        mid = bc[c // 2 - 1:c // 2, :]
        last = bc[c - 1:c, :]
        q_t = (q_dec * jnp.exp(-mid)).astype(BF16)
        k_t = (k_dec * jnp.exp(mid - last)).astype(BF16)
        s = lax.dot_general(q_t, k_t, (((1,), (1,)), ((), ())), preferred_element_type=F32)
        return jnp.where(causal, s, 0.0).astype(BF16)

    def exact_scores(ci, bc, q, k, q_dec, k_dec):
        kf_ref[...] = k

        def column(s, p):
            w = jnp.exp(jnp.minimum(bc - bc_ref[pl.ds(ci * c + s, 1), :], 0.0))
            val = jnp.sum(q * w * kf_ref[pl.ds(s, 1), :], axis=-1, keepdims=True)
            return jnp.where(col == s, val, p)

        p = lax.fori_loop(0, c, column, jnp.zeros((c, c), F32))
        return jnp.where(causal, p, 0.0).astype(BF16)

    def run_blocks(scores_fn):
        local = []
        for ci in range(n_blocks):
            rows = slice(ci * c, (ci + 1) * c)
            bc = bc_ref[rows, :]
            last = bc[c - 1:c, :]
            q = q_ref[0, rows, :].astype(F32)
            k = k_ref[0, rows, :].astype(F32)
            v = v_ref[0, rows, :]
            q_dec = q * jnp.exp(bc)
            k_dec = k * jnp.exp(last - bc)
            p = scores_fn(ci, bc, q, k, q_dec, k_dec)
            intra = jnp.dot(p, v, preferred_element_type=F32)
            update = lax.dot_general(v, k_dec.astype(BF16), (((0,), (0,)), ((), ())),
                                     preferred_element_type=F32)
            local.append((q_dec.astype(BF16), intra, update, jnp.exp(last)))
        for ci, (q_in, intra, update, decay) in enumerate(local):
            rows = slice(ci * c, (ci + 1) * c)
            st = state_ref[...]
            o = intra + lax.dot_general(q_in, st.astype(BF16), (((1,), (1,)), ((), ())),
                                        preferred_element_type=F32)
            state_ref[...] = st * decay + update
            o_ref[0, rows, :] = o.astype(o_ref.dtype)

    run_blocks(factored_scores)

    @pl.when(jnp.logical_not(jnp.min(worst) > -GLA_SAFE_LOG_DECAY))
    def _():
        state_ref[...] = state0_ref[...]
        run_blocks(exact_scores)


def _gla(proj, lr, wlr3, blr):
    b, s, _ = proj.shape
    qk_w = proj.shape[2] // 6
    v_w = 2 * qk_w
    dk, dv = qk_w // GLA_HEADS, v_w // GLA_HEADS
    ts = min(GLA_TS, s)
    nh = GLA_HEADS
    return pl.pallas_call(
        _gla_kernel,
        out_shape=jax.ShapeDtypeStruct((b, s, v_w), BF16),
        grid=(b, nh, s // ts),
        in_specs=[pl.BlockSpec((1, ts, dk), lambda bi, h, i: (bi, i, h)),
                  pl.BlockSpec((1, ts, dk), lambda bi, h, i: (bi, i, nh + h)),
                  pl.BlockSpec((1, ts, dv), lambda bi, h, i: (bi, i, nh + h)),
                  pl.BlockSpec((1, ts, LANES), lambda bi, h, i: (bi, i, 0)),
                  pl.BlockSpec((LANES, dk), lambda bi, h, i: (0, h)),
                  pl.BlockSpec((1, dk), lambda bi, h, i: (0, h))],
        out_specs=pl.BlockSpec((1, ts, dv), lambda bi, h, i: (bi, i, h)),
        scratch_shapes=[pltpu.VMEM((dv, dk), F32),
                        pltpu.VMEM((dv, dk), F32),
                        pltpu.VMEM((ts, dk), F32),
                        pltpu.VMEM((GLA_BLOCK, dk), F32)],
        compiler_params=_params("parallel", "parallel", "arbitrary"),
        name="gla",
    )(proj, proj, proj, lr, wlr3, blr)


def _pad_rows(a, rows):
    return jnp.pad(a, ((0, rows - a.shape[0]), (0, 0)))


def kernel(x, positions, ev_norm, ev_w_in, sgu_gain, sgu_w_s, sgu_b, ev_w_out, od_norm, od_w_in,
           gla_w_lr, gla_b_lr, od_w_out, final_norm):
    b, s, d = x.shape
    t = b * s
    x2 = x.reshape(t, d)

    h0 = _rmsnorm(x2, ev_norm[0])
    proj0 = _inproj(h0, ev_w_in, ev_w_in.shape[2]).reshape(b, s, -1)
    dk_ret = proj0.shape[2] // 12 // RET_HEADS
    half = dk_ret // 2
    inv = jnp.power(ROPE_BASE, -jnp.arange(half, dtype=F32) / half).reshape(1, half)
    posf = positions.astype(F32).reshape(b, s, 1)
    ret = _retention(proj0, posf, inv)
    w_out0 = ev_w_out[0].astype(BF16)
    kr = ret.shape[2]
    n_main = od_w_in.shape[2] - GLA_RANK
    w_lr_in = jnp.pad(jnp.tile(od_w_in[0, :, n_main:], (1, 3)), ((0, 0), (0, LANES - 3 * GLA_RANK)))
    x1, h1, lr = _outproj_mid(ret.reshape(t, kr), proj0.reshape(t, -1), x2, w_out0, od_norm[0],
                              w_lr_in.astype(BF16), sgu_gain[0], sgu_w_s[0], sgu_b[0],
                              dv=kr // RET_HEADS)

    proj1 = _inproj(h1, jnp.swapaxes(od_w_in, 1, 2), n_main, w_transposed=True)
    w_hi, w_lo = _split_bf16(gla_w_lr[0])
    wlr3 = _pad_rows(jnp.concatenate([w_hi, w_hi, w_lo], axis=0), LANES)
    attn = _gla(proj1.reshape(b, s, -1), lr.reshape(b, s, LANES), wlr3, gla_b_lr[0].reshape(1, -1))
    dk_gla = n_main // 6 // GLA_HEADS
    out = _outproj_final(attn.reshape(t, -1), proj1, x1, od_w_out[0].astype(BF16), final_norm,
                         dv=2 * dk_gla, head_eps=EPS * dk_gla)
    return out.reshape(b, s, d)
```

```python
import functools

import numpy as np
import jax
import jax.numpy as jnp
from jax import lax
from jax.experimental import pallas as pl
from jax.experimental.pallas import tpu as pltpu

F32 = jnp.float32
BF16 = jnp.bfloat16

EPS = 1e-6
RET_HEADS = 4
RET_BLOCK = 256
ROPE_BASE = 10000.0
SGU_GROUPS = 8
SGU_CHUNK = 128
GLA_HEADS = 8
GLA_RANK = 16
GLA_TAU = 16.0
GLA_BLOCK = 128
GLA_SAFE_LOG_DECAY = 60.0

LANES = 128
VMEM_LIMIT = 56 * 1024 * 1024

NORM_TM = 512
PROJ_TM = 1024
PROJ_TN = 1024
OUT_TM = 256
RET_TS = 1024
MID_TM = 256
GLA_TS = 1024


def _params(*sem):
    return pltpu.CompilerParams(dimension_semantics=sem, vmem_limit_bytes=VMEM_LIMIT)


def _silu(g):
    return g * (1.0 / (1.0 + jnp.exp(-g)))


def _rms(x, gain):
    return x * lax.rsqrt(jnp.mean(x * x, axis=-1, keepdims=True) + EPS) * gain


def _norm_proj_kernel(x_ref, g_ref, w_ref, h_ref, o_ref, wbf_ref):
    @pl.when(pl.program_id(0) == 0)
    def _():
        wbf_ref[...] = w_ref[...].astype(BF16)

    h = _rms(x_ref[...], g_ref[...]).astype(BF16)
    h_ref[...] = h
    o_ref[...] = jnp.dot(h, wbf_ref[...], preferred_element_type=F32).astype(o_ref.dtype)


def _norm_proj(x, gain, w, n_cols):
    t, d = x.shape
    tm = min(NORM_TM, t)
    last = w.shape[2] // n_cols - 1
    return pl.pallas_call(
        _norm_proj_kernel,
        out_shape=(jax.ShapeDtypeStruct((t, d), BF16), jax.ShapeDtypeStruct((t, n_cols), BF16)),
        grid=(t // tm,),
        in_specs=[pl.BlockSpec((tm, d), lambda i: (i, 0)),
                  pl.BlockSpec((1, d), lambda i: (0, 0)),
                  pl.BlockSpec((None, d, n_cols), lambda i: (0, 0, last), pipeline_mode=pl.Buffered(1))],
        out_specs=(pl.BlockSpec((tm, d), lambda i: (i, 0)), pl.BlockSpec((tm, n_cols), lambda i: (i, 0))),
        scratch_shapes=[pltpu.VMEM((d, n_cols), BF16)],
        compiler_params=_params("arbitrary"),
        name="norm_proj",
    )(x, gain.reshape(1, d), w)


def _inproj_kernel(h_ref, w_ref, o_ref, wbf_ref, *, w_transposed):
    @pl.when(pl.program_id(1) == 0)
    def _():
        wbf_ref[...] = w_ref[...].astype(BF16)

    contract_w = 1 if w_transposed else 0
    o_ref[...] = lax.dot_general(h_ref[...], wbf_ref[...], (((1,), (contract_w,)), ((), ())),
                                 preferred_element_type=F32).astype(o_ref.dtype)


def _inproj(h, w, n, w_transposed=False):
    t, d = h.shape
    tm, tn = min(PROJ_TM, t), min(PROJ_TN, n)
    if w_transposed:
        w_spec = pl.BlockSpec((None, tn, d), lambda j, i: (0, j, 0))
    else:
        w_spec = pl.BlockSpec((None, d, tn), lambda j, i: (0, 0, j))
    return pl.pallas_call(
        functools.partial(_inproj_kernel, w_transposed=w_transposed),
        out_shape=jax.ShapeDtypeStruct((t, n), BF16),
        grid=(n // tn, t // tm),
        in_specs=[pl.BlockSpec((tm, d), lambda j, i: (i, 0)), w_spec],
        out_specs=pl.BlockSpec((tm, tn), lambda j, i: (i, j)),
        scratch_shapes=[pltpu.VMEM((tn, d) if w_transposed else (d, tn), BF16)],
        compiler_params=_params("parallel", "arbitrary"),
        name="inproj",
    )(h, w)


def _retention_tables(dk):
    c = RET_BLOCK
    h = np.arange(RET_HEADS, dtype=np.float64)
    log_gamma = np.log1p(-np.power(2.0, -5.0 - h))
    idx = np.arange(c, dtype=np.float64)
    rel = idx[:, None] - idx[None, :]
    dmat = np.where(rel >= 0, np.exp(log_gamma[:, None, None] * np.maximum(rel, 0.0)), 0.0)
    qdec = np.exp(log_gamma[:, None] * (idx + 1.0))
    kdec = np.exp(log_gamma[:, None] * (c - 1.0 - idx))
    cdec = np.exp(log_gamma * c)
    lanes = lambda a: np.broadcast_to(a[:, :, None], (RET_HEADS, c, dk // 2))
    return (jnp.asarray(dmat, F32), jnp.asarray(lanes(qdec), F32), jnp.asarray(lanes(kdec), F32),
            [float(v) for v in cdec], float(dk) ** -0.5)


def _retention_kernel(q_ref, k_ref, v_ref, pos_ref, inv_ref, dmat_ref, qdec_ref, kdec_ref,
                      o_ref, state_ref, *, cdec, scale, dk, dv):
    c = RET_BLOCK
    half = dk // 2
    ts = q_ref.shape[1]

    @pl.when(pl.program_id(1) == 0)
    def _():
        state_ref[...] = jnp.zeros_like(state_ref)

    def chunk(ci, carry):
        rows = pl.ds(pl.multiple_of(ci * c, c), c)
        ang = pos_ref[0, rows, :] * inv_ref[...]
        cos, sin = jnp.cos(ang), jnp.sin(ang)
        cos_q, sin_q = cos * scale, sin * scale
        for h in range(RET_HEADS):
            q = q_ref[0, rows, h * dk:(h + 1) * dk].astype(F32)
            k = k_ref[0, rows, h * dk:(h + 1) * dk].astype(F32)
            q1, q2 = q[:, :half], q[:, half:]
            k1, k2 = k[:, :half], k[:, half:]
            qr1, qr2 = q1 * cos_q - q2 * sin_q, q1 * sin_q + q2 * cos_q
            kr1, kr2 = k1 * cos - k2 * sin, k1 * sin + k2 * cos
            qdec, kdec = qdec_ref[h], kdec_ref[h]
            qr = jnp.concatenate([qr1, qr2], axis=-1).astype(BF16)
            kr = jnp.concatenate([kr1, kr2], axis=-1).astype(BF16)
            qd = jnp.concatenate([qr1 * qdec, qr2 * qdec], axis=-1).astype(BF16)
            kd = jnp.concatenate([kr1 * kdec, kr2 * kdec], axis=-1).astype(BF16)
            s = lax.dot_general(qr, kr, (((1,), (1,)), ((), ())), preferred_element_type=F32)
            p = (s * dmat_ref[h]).astype(BF16)
            v = v_ref[0, rows, h * dv:(h + 1) * dv]
            st = state_ref[h]
            o = (jnp.dot(p, v, preferred_element_type=F32)
                 + jnp.dot(qd, st.astype(BF16), preferred_element_type=F32))
            state_ref[h] = cdec[h] * st + lax.dot_general(
                kd, v, (((0,), (0,)), ((), ())), preferred_element_type=F32)
            o_ref[0, rows, h * dv:(h + 1) * dv] = o.astype(o_ref.dtype)
        return carry

    lax.fori_loop(0, ts // c, chunk, 0)


def _retention(proj, posf, inv, qk_w):
    b, s, _ = proj.shape
    v_w = 2 * qk_w
    dk, dv = qk_w // RET_HEADS, v_w // RET_HEADS
    ts = min(RET_TS, s)
    dmat, qdec, kdec, cdec, scale = _retention_tables(dk)
    const3 = lambda shape: pl.BlockSpec(shape, lambda bi, i: (0, 0, 0))
    return pl.pallas_call(
        functools.partial(_retention_kernel, cdec=cdec, scale=scale, dk=dk, dv=dv),
        out_shape=jax.ShapeDtypeStruct((b, s, v_w), BF16),
        grid=(b, s // ts),
        in_specs=[pl.BlockSpec((1, ts, qk_w), lambda bi, i: (bi, i, 0)),
                  pl.BlockSpec((1, ts, qk_w), lambda bi, i: (bi, i, 1)),
                  pl.BlockSpec((1, ts, v_w), lambda bi, i: (bi, i, 1)),
                  pl.BlockSpec((1, ts, 1), lambda bi, i: (bi, i, 0)),
                  pl.BlockSpec((1, dk // 2), lambda bi, i: (0, 0)),
                  const3(dmat.shape), const3(qdec.shape), const3(kdec.shape)],
        out_specs=pl.BlockSpec((1, ts, v_w), lambda bi, i: (bi, i, 0)),
        scratch_shapes=[pltpu.VMEM((RET_HEADS, dk, dv), F32)],
        compiler_params=_params("parallel", "arbitrary"),
        name="retention",
    )(proj, proj, proj, posf, inv, dmat, qdec, kdec)


def _sgu_block(u_ref, v_ref, g_ref, gain_ref, ws_ref, sb_ref, sgu_ref, *, dg):
    c = SGU_CHUNK
    row = lax.broadcasted_iota(jnp.int32, (c, c), 0)
    col = lax.broadcasted_iota(jnp.int32, (c, c), 1)
    causal = row >= col
    wm = [jnp.where(causal, ws_ref[g], 0.0).astype(BF16) for g in range(SGU_GROUPS)]
    for ci in range(u_ref.shape[0] // c):
        rows = slice(ci * c, (ci + 1) * c)
        for g in range(SGU_GROUPS):
            cols = slice(g * dg, (g + 1) * dg)
            v = v_ref[rows, cols].astype(F32)
            d = v - jnp.mean(v, axis=-1, keepdims=True)
            var = jnp.mean(d * d, axis=-1, keepdims=True)
            vn = d * lax.rsqrt(var + EPS) * gain_ref[:, cols]
            sg = jnp.dot(wm[g], vn.astype(BF16), preferred_element_type=F32) + sb_ref[g]
            sgu_ref[rows, cols] = sg.astype(BF16) * (u_ref[rows, cols] * _silu(g_ref[rows, cols]))


def _gated_heads_matmul(acc, a_ref, g_ref, w_ref, dv, head_eps):
    for h in range(a_ref.shape[1] // dv):
        cols = slice(h * dv, (h + 1) * dv)
        o = a_ref[:, cols].astype(F32)
        on = o * lax.rsqrt(jnp.mean(o * o, axis=-1, keepdims=True) + head_eps)
        mixed = on.astype(BF16) * _silu(g_ref[:, cols])
        acc = acc + jnp.dot(mixed, w_ref[cols, :], preferred_element_type=F32)
    return acc


def _outproj_mid_kernel(a_ref, ga_ref, u_ref, v_ref, g_ref, x_ref, wa_ref, wb_ref, gn_ref, wlr_ref,
                        sgain_ref, ws_ref, sb_ref, x1_ref, h1_ref, lr_ref, sgu_ref, *, dg, dv):
    acc = _gated_heads_matmul(x_ref[...], a_ref, ga_ref, wa_ref, dv, EPS)
    _sgu_block(u_ref, v_ref, g_ref, sgain_ref, ws_ref, sb_ref, sgu_ref, dg=dg)
    x1 = acc + jnp.dot(sgu_ref[...], wb_ref[...], preferred_element_type=F32)
    x1_ref[...] = x1
    h1 = _rms(x1, gn_ref[...]).astype(h1_ref.dtype)
    h1_ref[...] = h1
    lr_ref[...] = jnp.dot(h1, wlr_ref[...], preferred_element_type=F32)


def _outproj_final_kernel(a_ref, g_ref, x_ref, w_ref, gn_ref, o_ref, *, dv, head_eps):
    x1 = _gated_heads_matmul(x_ref[...], a_ref, g_ref, w_ref, dv, head_eps)
    o_ref[...] = _rms(x1, gn_ref[...])


def _resident(shape):
    return pl.BlockSpec(shape, lambda i: (0, 0), pipeline_mode=pl.Buffered(1))


def _outproj_mid(a, proj, g_s, x, w, gain, wlr, sgu_gain, w_s, b_s, dv):
    t, d = x.shape
    ka = a.shape[1]
    kb = g_s.shape[1]
    assert ka == kb and w.shape == (ka + kb, d)
    dg = kb // SGU_GROUPS
    tm = min(MID_TM, t)
    bias = jnp.broadcast_to(b_s[:, :, None], (SGU_GROUPS, SGU_CHUNK, dg))
    row = lambda w_, blk=0: pl.BlockSpec((tm, w_), lambda i: (i, blk))
    w_half = lambda blk: pl.BlockSpec((ka, d), lambda i: (blk, 0), pipeline_mode=pl.Buffered(1))
    const3 = lambda shape: pl.BlockSpec(shape, lambda i: (0, 0, 0), pipeline_mode=pl.Buffered(1))
    return pl.pallas_call(
        functools.partial(_outproj_mid_kernel, dg=dg, dv=dv),
        out_shape=(jax.ShapeDtypeStruct((t, d), F32), jax.ShapeDtypeStruct((t, d), BF16),
                   jax.ShapeDtypeStruct((t, LANES), F32)),
        grid=(t // tm,),
        in_specs=[row(ka), row(ka, 2), row(kb, 3), row(kb, 4), row(kb), row(d), w_half(0), w_half(1),
                  _resident((1, d)), _resident((d, LANES)), _resident((1, kb)),
                  const3(w_s.shape), const3(bias.shape)],
        out_specs=(row(d), row(d), row(LANES)),
        scratch_shapes=[pltpu.VMEM((tm, kb), BF16)],
        compiler_params=_params("parallel"),
        name="outproj_mid",
    )(a, proj, proj, proj, g_s, x, w, w, gain.reshape(1, d), wlr, sgu_gain.reshape(1, kb), w_s, bias)


def _outproj_final(a, proj, x, w, gain, dv, head_eps):
    t, d = x.shape
    ka = a.shape[1]
    tm = min(OUT_TM, t)
    row = lambda w_, blk=0: pl.BlockSpec((tm, w_), lambda i: (i, blk))
    return pl.pallas_call(
        functools.partial(_outproj_final_kernel, dv=dv, head_eps=head_eps),
        out_shape=jax.ShapeDtypeStruct((t, d), F32),
        grid=(t // tm,),
        in_specs=[row(ka), row(ka, 2), row(d), _resident((ka, d)), _resident((1, d))],
        out_specs=row(d),
        compiler_params=_params("parallel"),
        name="outproj_final",
    )(a, proj, x, w, gain.reshape(1, d))


def _split_bf16(x):
    hi = x.astype(BF16)
    lo = (x - hi.astype(F32)).astype(BF16)
    return hi, lo


def _gla_kernel(q_ref, k_ref, v_ref, lr_ref, wlr_ref, blr_ref, o_ref,
                state_ref, state0_ref, bc_ref, kf_ref):
    c = GLA_BLOCK
    ts = q_ref.shape[1]
    n_blocks = ts // c

    @pl.when(pl.program_id(2) == 0)
    def _():
        state_ref[...] = jnp.zeros_like(state_ref)

    state0_ref[...] = state_ref[...]

    lr_hi, lr_lo = _split_bf16(lr_ref[0])
    lane = lax.broadcasted_iota(jnp.int32, lr_hi.shape, 1)
    lr3 = jnp.where((lane >= GLA_RANK) & (lane < 2 * GLA_RANK), lr_lo, lr_hi)
    z = jnp.dot(lr3, wlr_ref[...], preferred_element_type=F32) + blr_ref[...]
    log_alpha = (jnp.minimum(z, 0.0) - jnp.log(1.0 + jnp.exp(-jnp.abs(z)))) * (1.0 / GLA_TAU)

    row = lax.broadcasted_iota(jnp.int32, (c, c), 0)
    col = lax.broadcasted_iota(jnp.int32, (c, c), 1)
    causal = row >= col
    tri = jnp.where(causal, 1.0, 0.0).astype(BF16)
    tri2 = jnp.concatenate([tri, tri], axis=1)

    worst = None
    for ci in range(n_blocks):
        rows = slice(ci * c, (ci + 1) * c)
        la_hi, la_lo = _split_bf16(log_alpha[rows])
        bc = jnp.dot(tri2, jnp.concatenate([la_hi, la_lo], axis=0), preferred_element_type=F32)
        bc_ref[rows, :] = bc
        worst = bc[c - 1:c, :] if worst is None else jnp.minimum(worst, bc[c - 1:c, :])

    def factored_scores(ci, bc, q, k, q_dec, k_dec):
        mid = bc[c // 2 - 1:c // 2, :]
        last = bc[c - 1:c, :]
        q_t = (q_dec * jnp.exp(-mid)).astype(BF16)
        k_t = (k_dec * jnp.exp(mid - last)).astype(BF16)
        s = lax.dot_general(q_t, k_t, (((1,), (1,)), ((), ())), preferred_element_type=F32)
        return jnp.where(causal, s, 0.0).astype(BF16)

    def exact_scores(ci, bc, q, k, q_dec, k_dec):
        kf_ref[...] = k

        def column(s, p):
            w = jnp.exp(jnp.minimum(bc - bc_ref[pl.ds(ci * c + s, 1), :], 0.0))
            val = jnp.sum(q * w * kf_ref[pl.ds(s, 1), :], axis=-1, keepdims=True)
            return jnp.where(col == s, val, p)

        p = lax.fori_loop(0, c, column, jnp.zeros((c, c), F32))
        return jnp.where(causal, p, 0.0).astype(BF16)

    def run_blocks(scores_fn):
        local = []
        for ci in range(n_blocks):
            rows = slice(ci * c, (ci + 1) * c)
            bc = bc_ref[rows, :]
            last = bc[c - 1:c, :]
            q = q_ref[0, rows, :].astype(F32)
            k = k_ref[0, rows, :].astype(F32)
            v = v_ref[0, rows, :]
            q_dec = q * jnp.exp(bc)
            k_dec = k * jnp.exp(last - bc)
            p = scores_fn(ci, bc, q, k, q_dec, k_dec)
            intra = jnp.dot(p, v, preferred_element_type=F32)
            update = lax.dot_general(v, k_dec.astype(BF16), (((0,), (0,)), ((), ())),
                                     preferred_element_type=F32)
            local.append((q_dec.astype(BF16), intra, update, jnp.exp(last)))
        for ci, (q_in, intra, update, decay) in enumerate(local):
            rows = slice(ci * c, (ci + 1) * c)
            st = state_ref[...]
            o = intra + lax.dot_general(q_in, st.astype(BF16), (((1,), (1,)), ((), ())),
                                        preferred_element_type=F32)
            state_ref[...] = st * decay + update
            o_ref[0, rows, :] = o.astype(o_ref.dtype)

    run_blocks(factored_scores)

    @pl.when(jnp.logical_not(jnp.min(worst) > -GLA_SAFE_LOG_DECAY))
    def _():
        state_ref[...] = state0_ref[...]
        run_blocks(exact_scores)


def _gla(proj, lr, wlr3, blr):
    b, s, _ = proj.shape
    qk_w = proj.shape[2] // 6
    v_w = 2 * qk_w
    dk, dv = qk_w // GLA_HEADS, v_w // GLA_HEADS
    ts = min(GLA_TS, s)
    nh = GLA_HEADS
    return pl.pallas_call(
        _gla_kernel,
        out_shape=jax.ShapeDtypeStruct((b, s, v_w), BF16),
        grid=(b, nh, s // ts),
        in_specs=[pl.BlockSpec((1, ts, dk), lambda bi, h, i: (bi, i, h)),
                  pl.BlockSpec((1, ts, dk), lambda bi, h, i: (bi, i, nh + h)),
                  pl.BlockSpec((1, ts, dv), lambda bi, h, i: (bi, i, nh + h)),
                  pl.BlockSpec((1, ts, LANES), lambda bi, h, i: (bi, i, 0)),
                  pl.BlockSpec((LANES, dk), lambda bi, h, i: (0, h)),
                  pl.BlockSpec((1, dk), lambda bi, h, i: (0, h))],
        out_specs=pl.BlockSpec((1, ts, dv), lambda bi, h, i: (bi, i, h)),
        scratch_shapes=[pltpu.VMEM((dv, dk), F32),
                        pltpu.VMEM((dv, dk), F32),
                        pltpu.VMEM((ts, dk), F32),
                        pltpu.VMEM((GLA_BLOCK, dk), F32)],
        compiler_params=_params("parallel", "parallel", "arbitrary"),
        name="gla",
    )(proj, proj, proj, lr, wlr3, blr)


def _pad_rows(a, rows):
    return jnp.pad(a, ((0, rows - a.shape[0]), (0, 0)))


def kernel(x, positions, ev_norm, ev_w_in, sgu_gain, sgu_w_s, sgu_b, ev_w_out, od_norm, od_w_in,
           gla_w_lr, gla_b_lr, od_w_out, final_norm):
    b, s, d = x.shape
    t = b * s
    x2 = x.reshape(t, d)

    qk_w = ev_w_in.shape[2] // 12
    h0, g_s = _norm_proj(x2, ev_norm[0], ev_w_in, 2 * qk_w)
    proj0 = _inproj(h0, ev_w_in, 10 * qk_w).reshape(b, s, -1)
    half = qk_w // RET_HEADS // 2
    inv = jnp.power(ROPE_BASE, -jnp.arange(half, dtype=F32) / half).reshape(1, half)
    posf = positions.astype(F32).reshape(b, s, 1)
    ret = _retention(proj0, posf, inv, qk_w)
    w_out0 = ev_w_out[0].astype(BF16)
    kr = ret.shape[2]
    n_main = od_w_in.shape[2] - GLA_RANK
    w_lr_in = jnp.pad(jnp.tile(od_w_in[0, :, n_main:], (1, 3)), ((0, 0), (0, LANES - 3 * GLA_RANK)))
    x1, h1, lr = _outproj_mid(ret.reshape(t, kr), proj0.reshape(t, -1), g_s, x2, w_out0, od_norm[0],
                              w_lr_in.astype(BF16), sgu_gain[0], sgu_w_s[0], sgu_b[0],
                              dv=kr // RET_HEADS)

    proj1 = _inproj(h1, jnp.swapaxes(od_w_in, 1, 2), n_main, w_transposed=True)
    w_hi, w_lo = _split_bf16(gla_w_lr[0])
    wlr3 = _pad_rows(jnp.concatenate([w_hi, w_hi, w_lo], axis=0), LANES)
    attn = _gla(proj1.reshape(b, s, -1), lr.reshape(b, s, LANES), wlr3, gla_b_lr[0].reshape(1, -1))
    dk_gla = n_main // 6 // GLA_HEADS
    out = _outproj_final(attn.reshape(t, -1), proj1, x1, od_w_out[0].astype(BF16), final_norm,
                         dv=2 * dk_gla, head_eps=EPS * dk_gla)
    return out.reshape(b, s, d)
```

```python
import functools

import numpy as np
import jax
import jax.numpy as jnp
from jax import lax
from jax.experimental import pallas as pl
from jax.experimental.pallas import tpu as pltpu

F32 = jnp.float32
BF16 = jnp.bfloat16

EPS = 1e-6
RET_HEADS = 4
RET_BLOCK = 256
ROPE_BASE = 10000.0
SGU_GROUPS = 8
SGU_CHUNK = 128
GLA_HEADS = 8
GLA_RANK = 16
GLA_TAU = 16.0
GLA_BLOCK = 128
GLA_SAFE_LOG_DECAY = 60.0

LANES = 128
VMEM_LIMIT = 56 * 1024 * 1024

NORM_TM = 512
PROJ_TM = 1024
PROJ_TN = 1024
OUT_TM = 256
RET_TS = 1024
MID_TM = 256
GLA_TS = 1024


def _params(*sem):
    return pltpu.CompilerParams(dimension_semantics=sem, vmem_limit_bytes=VMEM_LIMIT)


def _silu(g):
    return g * (1.0 / (1.0 + jnp.exp(-g)))


def _rms(x, gain):
    return x * lax.rsqrt(jnp.mean(x * x, axis=-1, keepdims=True) + EPS) * gain


def _norm_proj_kernel(x_ref, g_ref, w_ref, pos_ref, inv_ref, h_ref, o_ref, cos_ref, sin_ref, wbf_ref):
    @pl.when(pl.program_id(0) == 0)
    def _():
        wbf_ref[...] = w_ref[...].astype(BF16)

    h = _rms(x_ref[...], g_ref[...]).astype(BF16)
    h_ref[...] = h
    o_ref[...] = jnp.dot(h, wbf_ref[...], preferred_element_type=F32).astype(o_ref.dtype)
    ang = pos_ref[...] * inv_ref[...]
    cos_ref[...] = jnp.cos(ang)
    sin_ref[...] = jnp.sin(ang)


def _norm_proj(x, gain, w, n_cols, posf, inv):
    t, d = x.shape
    half = inv.shape[1]
    tm = min(NORM_TM, t)
    last = w.shape[2] // n_cols - 1
    row = lambda w_: pl.BlockSpec((tm, w_), lambda i: (i, 0))
    return pl.pallas_call(
        _norm_proj_kernel,
        out_shape=(jax.ShapeDtypeStruct((t, d), BF16), jax.ShapeDtypeStruct((t, n_cols), BF16),
                   jax.ShapeDtypeStruct((t, half), F32), jax.ShapeDtypeStruct((t, half), F32)),
        grid=(t // tm,),
        in_specs=[row(d),
                  pl.BlockSpec((1, d), lambda i: (0, 0)),
                  pl.BlockSpec((None, d, n_cols), lambda i: (0, 0, last), pipeline_mode=pl.Buffered(1)),
                  row(1),
                  pl.BlockSpec((1, half), lambda i: (0, 0))],
        out_specs=(row(d), row(n_cols), row(half), row(half)),
        scratch_shapes=[pltpu.VMEM((d, n_cols), BF16)],
        compiler_params=_params("arbitrary"),
        name="norm_proj",
    )(x, gain.reshape(1, d), w, posf, inv)


def _inproj_kernel(h_ref, w_ref, o_ref, wbf_ref, *, w_transposed):
    @pl.when(pl.program_id(1) == 0)
    def _():
        wbf_ref[...] = w_ref[...].astype(BF16)

    contract_w = 1 if w_transposed else 0
    o_ref[...] = lax.dot_general(h_ref[...], wbf_ref[...], (((1,), (contract_w,)), ((), ())),
                                 preferred_element_type=F32).astype(o_ref.dtype)


def _inproj(h, w, n, w_transposed=False):
    t, d = h.shape
    tm, tn = min(PROJ_TM, t), min(PROJ_TN, n)
    if w_transposed:
        w_spec = pl.BlockSpec((None, tn, d), lambda j, i: (0, j, 0))
    else:
        w_spec = pl.BlockSpec((None, d, tn), lambda j, i: (0, 0, j))
    return pl.pallas_call(
        functools.partial(_inproj_kernel, w_transposed=w_transposed),
        out_shape=jax.ShapeDtypeStruct((t, n), BF16),
        grid=(n // tn, t // tm),
        in_specs=[pl.BlockSpec((tm, d), lambda j, i: (i, 0)), w_spec],
        out_specs=pl.BlockSpec((tm, tn), lambda j, i: (i, j)),
        scratch_shapes=[pltpu.VMEM((tn, d) if w_transposed else (d, tn), BF16)],
        compiler_params=_params("parallel", "arbitrary"),
        name="inproj",
    )(h, w)


def _retention_tables(dk):
    c = RET_BLOCK
    h = np.arange(RET_HEADS, dtype=np.float64)
    log_gamma = np.log1p(-np.power(2.0, -5.0 - h))
    idx = np.arange(c, dtype=np.float64)
    rel = idx[:, None] - idx[None, :]
    dmat = np.where(rel >= 0, np.exp(log_gamma[:, None, None] * np.maximum(rel, 0.0)), 0.0)
    qdec = np.exp(log_gamma[:, None] * (idx + 1.0))
    kdec = np.exp(log_gamma[:, None] * (c - 1.0 - idx))
    cdec = np.exp(log_gamma * c)
    lanes = lambda a: np.broadcast_to(a[:, :, None], (RET_HEADS, c, dk // 2))
    return (jnp.asarray(dmat, F32), jnp.asarray(lanes(qdec), F32), jnp.asarray(lanes(kdec), F32),
            [float(v) for v in cdec], float(dk) ** -0.5)


def _retention_kernel(q_ref, k_ref, v_ref, cos_ref, sin_ref, dmat_ref, qdec_ref, kdec_ref,
                      o_ref, state_ref, *, cdec, scale, dk, dv):
    c = RET_BLOCK
    half = dk // 2
    ts = q_ref.shape[1]

    @pl.when(pl.program_id(1) == 0)
    def _():
        state_ref[...] = jnp.zeros_like(state_ref)

    def chunk(ci, carry):
        rows = pl.ds(pl.multiple_of(ci * c, c), c)
        cos, sin = cos_ref[0, rows, :], sin_ref[0, rows, :]
        cos_q, sin_q = cos * scale, sin * scale
        for h in range(RET_HEADS):
            q = q_ref[0, rows, h * dk:(h + 1) * dk].astype(F32)
            k = k_ref[0, rows, h * dk:(h + 1) * dk].astype(F32)
            q1, q2 = q[:, :half], q[:, half:]
            k1, k2 = k[:, :half], k[:, half:]
            qr1, qr2 = q1 * cos_q - q2 * sin_q, q1 * sin_q + q2 * cos_q
            kr1, kr2 = k1 * cos - k2 * sin, k1 * sin + k2 * cos
            qdec, kdec = qdec_ref[h], kdec_ref[h]
            qr = jnp.concatenate([qr1, qr2], axis=-1).astype(BF16)
            kr = jnp.concatenate([kr1, kr2], axis=-1).astype(BF16)
            qd = jnp.concatenate([qr1 * qdec, qr2 * qdec], axis=-1).astype(BF16)
            kd = jnp.concatenate([kr1 * kdec, kr2 * kdec], axis=-1).astype(BF16)
            s = lax.dot_general(qr, kr, (((1,), (1,)), ((), ())), preferred_element_type=F32)
            p = (s * dmat_ref[h]).astype(BF16)
            v = v_ref[0, rows, h * dv:(h + 1) * dv]
            st = state_ref[h]
            o = (jnp.dot(p, v, preferred_element_type=F32)
                 + jnp.dot(qd, st.astype(BF16), preferred_element_type=F32))
            state_ref[h] = cdec[h] * st + lax.dot_general(
                kd, v, (((0,), (0,)), ((), ())), preferred_element_type=F32)
            o_ref[0, rows, h * dv:(h + 1) * dv] = o.astype(o_ref.dtype)
        return carry

    lax.fori_loop(0, ts // c, chunk, 0)


def _retention(proj, cos, sin, qk_w):
    b, s, _ = proj.shape
    v_w = 2 * qk_w
    dk, dv = qk_w // RET_HEADS, v_w // RET_HEADS
    ts = min(RET_TS, s)
    dmat, qdec, kdec, cdec, scale = _retention_tables(dk)
    const3 = lambda shape: pl.BlockSpec(shape, lambda bi, i: (0, 0, 0))
    return pl.pallas_call(
        functools.partial(_retention_kernel, cdec=cdec, scale=scale, dk=dk, dv=dv),
        out_shape=jax.ShapeDtypeStruct((b, s, v_w), BF16),
        grid=(b, s // ts),
        in_specs=[pl.BlockSpec((1, ts, qk_w), lambda bi, i: (bi, i, 0)),
                  pl.BlockSpec((1, ts, qk_w), lambda bi, i: (bi, i, 1)),
                  pl.BlockSpec((1, ts, v_w), lambda bi, i: (bi, i, 1)),
                  pl.BlockSpec((1, ts, dk // 2), lambda bi, i: (bi, i, 0)),
                  pl.BlockSpec((1, ts, dk // 2), lambda bi, i: (bi, i, 0)),
                  const3(dmat.shape), const3(qdec.shape), const3(kdec.shape)],
        out_specs=pl.BlockSpec((1, ts, v_w), lambda bi, i: (bi, i, 0)),
        scratch_shapes=[pltpu.VMEM((RET_HEADS, dk, dv), F32)],
        compiler_params=_params("parallel", "arbitrary"),
        name="retention",
    )(proj, proj, proj, cos, sin, dmat, qdec, kdec)


def _sgu_block(u_ref, v_ref, g_ref, gain_ref, ws_ref, sb_ref, sgu_ref, *, dg):
    c = SGU_CHUNK
    row = lax.broadcasted_iota(jnp.int32, (c, c), 0)
    col = lax.broadcasted_iota(jnp.int32, (c, c), 1)
    causal = row >= col
    wm = [jnp.where(causal, ws_ref[g], 0.0).astype(BF16) for g in range(SGU_GROUPS)]
    for ci in range(u_ref.shape[0] // c):
        rows = slice(ci * c, (ci + 1) * c)
        for g in range(SGU_GROUPS):
            cols = slice(g * dg, (g + 1) * dg)
            v = v_ref[rows, cols].astype(F32)
            d = v - jnp.mean(v, axis=-1, keepdims=True)
            var = jnp.mean(d * d, axis=-1, keepdims=True)
            vn = d * lax.rsqrt(var + EPS) * gain_ref[:, cols]
            sg = jnp.dot(wm[g], vn.astype(BF16), preferred_element_type=F32) + sb_ref[g]
            sgu_ref[rows, cols] = sg.astype(BF16) * (u_ref[rows, cols] * _silu(g_ref[rows, cols]))


def _gated_heads_matmul(acc, a_ref, g_ref, w_ref, dv, head_eps):
    for h in range(a_ref.shape[1] // dv):
        cols = slice(h * dv, (h + 1) * dv)
        o = a_ref[:, cols].astype(F32)
        on = o * lax.rsqrt(jnp.mean(o * o, axis=-1, keepdims=True) + head_eps)
        mixed = on.astype(BF16) * _silu(g_ref[:, cols])
        acc = acc + jnp.dot(mixed, w_ref[cols, :], preferred_element_type=F32)
    return acc


def _outproj_mid_kernel(a_ref, ga_ref, u_ref, v_ref, g_ref, x_ref, wa_ref, wb_ref, gn_ref, wlr_ref,
                        sgain_ref, ws_ref, sb_ref, x1_ref, h1_ref, lr_ref, sgu_ref, *, dg, dv):
    acc = _gated_heads_matmul(x_ref[...], a_ref, ga_ref, wa_ref, dv, EPS)
    _sgu_block(u_ref, v_ref, g_ref, sgain_ref, ws_ref, sb_ref, sgu_ref, dg=dg)
    x1 = acc + jnp.dot(sgu_ref[...], wb_ref[...], preferred_element_type=F32)
    x1_ref[...] = x1
    h1 = _rms(x1, gn_ref[...]).astype(h1_ref.dtype)
    h1_ref[...] = h1
    lr_ref[...] = jnp.dot(h1, wlr_ref[...], preferred_element_type=F32)


def _outproj_final_kernel(a_ref, g_ref, x_ref, w_ref, gn_ref, o_ref, *, dv, head_eps):
    x1 = _gated_heads_matmul(x_ref[...], a_ref, g_ref, w_ref, dv, head_eps)
    o_ref[...] = _rms(x1, gn_ref[...])


def _resident(shape):
    return pl.BlockSpec(shape, lambda i: (0, 0), pipeline_mode=pl.Buffered(1))


def _outproj_mid(a, proj, g_s, x, w, gain, wlr, sgu_gain, w_s, b_s, dv):
    t, d = x.shape
    ka = a.shape[1]
    kb = g_s.shape[1]
    assert ka == kb and w.shape == (ka + kb, d)
    dg = kb // SGU_GROUPS
    tm = min(MID_TM, t)
    bias = jnp.broadcast_to(b_s[:, :, None], (SGU_GROUPS, SGU_CHUNK, dg))
    row = lambda w_, blk=0: pl.BlockSpec((tm, w_), lambda i: (i, blk))
    w_half = lambda blk: pl.BlockSpec((ka, d), lambda i: (blk, 0), pipeline_mode=pl.Buffered(1))
    const3 = lambda shape: pl.BlockSpec(shape, lambda i: (0, 0, 0), pipeline_mode=pl.Buffered(1))
    return pl.pallas_call(
        functools.partial(_outproj_mid_kernel, dg=dg, dv=dv),
        out_shape=(jax.ShapeDtypeStruct((t, d), F32), jax.ShapeDtypeStruct((t, d), BF16),
                   jax.ShapeDtypeStruct((t, LANES), F32)),
        grid=(t // tm,),
        in_specs=[row(ka), row(ka, 2), row(kb, 3), row(kb, 4), row(kb), row(d), w_half(0), w_half(1),
                  _resident((1, d)), _resident((d, LANES)), _resident((1, kb)),
                  const3(w_s.shape), const3(bias.shape)],
        out_specs=(row(d), row(d), row(LANES)),
        scratch_shapes=[pltpu.VMEM((tm, kb), BF16)],
        compiler_params=_params("parallel"),
        name="outproj_mid",
    )(a, proj, proj, proj, g_s, x, w, w, gain.reshape(1, d), wlr, sgu_gain.reshape(1, kb), w_s, bias)


def _outproj_final(a, proj, x, w, gain, dv, head_eps):
    t, d = x.shape
    ka = a.shape[1]
    tm = min(OUT_TM, t)
    row = lambda w_, blk=0: pl.BlockSpec((tm, w_), lambda i: (i, blk))
    return pl.pallas_call(
        functools.partial(_outproj_final_kernel, dv=dv, head_eps=head_eps),
        out_shape=jax.ShapeDtypeStruct((t, d), F32),
        grid=(t // tm,),
        in_specs=[row(ka), row(ka, 2), row(d), _resident((ka, d)), _resident((1, d))],
        out_specs=row(d),
        compiler_params=_params("parallel"),
        name="outproj_final",
    )(a, proj, x, w, gain.reshape(1, d))


def _split_bf16(x):
    hi = x.astype(BF16)
    lo = (x - hi.astype(F32)).astype(BF16)
    return hi, lo


def _gla_kernel(q_ref, k_ref, v_ref, lr_ref, wlr_ref, blr_ref, o_ref,
                state_ref, state0_ref, bc_ref, kf_ref):
    c = GLA_BLOCK
    ts = q_ref.shape[1]
    n_blocks = ts // c

    @pl.when(pl.program_id(2) == 0)
    def _():
        state_ref[...] = jnp.zeros_like(state_ref)

    state0_ref[...] = state_ref[...]

    lr_hi, lr_lo = _split_bf16(lr_ref[0])
    lane = lax.broadcasted_iota(jnp.int32, lr_hi.shape, 1)
    lr3 = jnp.where((lane >= GLA_RANK) & (lane < 2 * GLA_RANK), lr_lo, lr_hi)
    z = jnp.dot(lr3, wlr_ref[...], preferred_element_type=F32) + blr_ref[...]
    log_alpha = (jnp.minimum(z, 0.0) - jnp.log(1.0 + jnp.exp(-jnp.abs(z)))) * (1.0 / GLA_TAU)

    row = lax.broadcasted_iota(jnp.int32, (c, c), 0)
    col = lax.broadcasted_iota(jnp.int32, (c, c), 1)
    causal = row >= col
    tri = jnp.where(causal, 1.0, 0.0).astype(BF16)
    tri2 = jnp.concatenate([tri, tri], axis=1)

    worst = None
    for ci in range(n_blocks):
        rows = slice(ci * c, (ci + 1) * c)
        la_hi, la_lo = _split_bf16(log_alpha[rows])
        bc = jnp.dot(tri2, jnp.concatenate([la_hi, la_lo], axis=0), preferred_element_type=F32)
        bc_ref[rows, :] = bc
        worst = bc[c - 1:c, :] if worst is None else jnp.minimum(worst, bc[c - 1:c, :])

    def factored_scores(ci, bc, q, k, q_dec, k_dec):
        mid = bc[c // 2 - 1:c // 2, :]
        last = bc[c - 1:c, :]
        q_t = (q_dec * jnp.exp(-mid)).astype(BF16)
        k_t = (k_dec * jnp.exp(mid - last)).astype(BF16)
        s = lax.dot_general(q_t, k_t, (((1,), (1,)), ((), ())), preferred_element_type=F32)
        return jnp.where(causal, s, 0.0).astype(BF16)

    def exact_scores(ci, bc, q, k, q_dec, k_dec):
        kf_ref[...] = k

        def column(s, p):
            w = jnp.exp(jnp.minimum(bc - bc_ref[pl.ds(ci * c + s, 1), :], 0.0))
            val = jnp.sum(q * w * kf_ref[pl.ds(s, 1), :], axis=-1, keepdims=True)
            return jnp.where(col == s, val, p)

        p = lax.fori_loop(0, c, column, jnp.zeros((c, c), F32))
        return jnp.where(causal, p, 0.0).astype(BF16)

    def run_blocks(scores_fn):
        local = []
        for ci in range(n_blocks):
            rows = slice(ci * c, (ci + 1) * c)
            bc = bc_ref[rows, :]
            last = bc[c - 1:c, :]
            q = q_ref[0, rows, :].astype(F32)
            k = k_ref[0, rows, :].astype(F32)
            v = v_ref[0, rows, :]
            q_dec = q * jnp.exp(bc)
            k_dec = k * jnp.exp(last - bc)
            p = scores_fn(ci, bc, q, k, q_dec, k_dec)
            intra = jnp.dot(p, v, preferred_element_type=F32)
            update = lax.dot_general(v, k_dec.astype(BF16), (((0,), (0,)), ((), ())),
                                     preferred_element_type=F32)
            local.append((q_dec.astype(BF16), intra, update, jnp.exp(last)))
        for ci, (q_in, intra, update, decay) in enumerate(local):
            rows = slice(ci * c, (ci + 1) * c)
            st = state_ref[...]
            o = intra + lax.dot_general(q_in, st.astype(BF16), (((1,), (1,)), ((), ())),
                                        preferred_element_type=F32)
            state_ref[...] = st * decay + update
            o_ref[0, rows, :] = o.astype(o_ref.dtype)

    run_blocks(factored_scores)

    @pl.when(jnp.logical_not(jnp.min(worst) > -GLA_SAFE_LOG_DECAY))
    def _():
        state_ref[...] = state0_ref[...]
        run_blocks(exact_scores)


def _gla(proj, lr, wlr3, blr):
    b, s, _ = proj.shape
    qk_w = proj.shape[2] // 6
    v_w = 2 * qk_w
    dk, dv = qk_w // GLA_HEADS, v_w // GLA_HEADS
    ts = min(GLA_TS, s)
    nh = GLA_HEADS
    return pl.pallas_call(
        _gla_kernel,
        out_shape=jax.ShapeDtypeStruct((b, s, v_w), BF16),
        grid=(b, nh, s // ts),
        in_specs=[pl.BlockSpec((1, ts, dk), lambda bi, h, i: (bi, i, h)),
                  pl.BlockSpec((1, ts, dk), lambda bi, h, i: (bi, i, nh + h)),
                  pl.BlockSpec((1, ts, dv), lambda bi, h, i: (bi, i, nh + h)),
                  pl.BlockSpec((1, ts, LANES), lambda bi, h, i: (bi, i, 0)),
                  pl.BlockSpec((LANES, dk), lambda bi, h, i: (0, h)),
                  pl.BlockSpec((1, dk), lambda bi, h, i: (0, h))],
        out_specs=pl.BlockSpec((1, ts, dv), lambda bi, h, i: (bi, i, h)),
        scratch_shapes=[pltpu.VMEM((dv, dk), F32),
                        pltpu.VMEM((dv, dk), F32),
                        pltpu.VMEM((ts, dk), F32),
                        pltpu.VMEM((GLA_BLOCK, dk), F32)],
        compiler_params=_params("parallel", "parallel", "arbitrary"),
        name="gla",
    )(proj, proj, proj, lr, wlr3, blr)


def _pad_rows(a, rows):
    return jnp.pad(a, ((0, rows - a.shape[0]), (0, 0)))


def kernel(x, positions, ev_norm, ev_w_in, sgu_gain, sgu_w_s, sgu_b, ev_w_out, od_norm, od_w_in,
           gla_w_lr, gla_b_lr, od_w_out, final_norm):
    b, s, d = x.shape
    t = b * s
    x2 = x.reshape(t, d)

    qk_w = ev_w_in.shape[2] // 12
    half = qk_w // RET_HEADS // 2
    inv = jnp.power(ROPE_BASE, -jnp.arange(half, dtype=F32) / half).reshape(1, half)
    posf = positions.astype(F32).reshape(t, 1)
    h0, g_s, cos, sin = _norm_proj(x2, ev_norm[0], ev_w_in, 2 * qk_w, posf, inv)
    proj0 = _inproj(h0, ev_w_in, 10 * qk_w).reshape(b, s, -1)
    ret = _retention(proj0, cos.reshape(b, s, half), sin.reshape(b, s, half), qk_w)
    w_out0 = ev_w_out[0].astype(BF16)
    kr = ret.shape[2]
    n_main = od_w_in.shape[2] - GLA_RANK
    w_lr_in = jnp.pad(jnp.tile(od_w_in[0, :, n_main:], (1, 3)), ((0, 0), (0, LANES - 3 * GLA_RANK)))
    x1, h1, lr = _outproj_mid(ret.reshape(t, kr), proj0.reshape(t, -1), g_s, x2, w_out0, od_norm[0],
                              w_lr_in.astype(BF16), sgu_gain[0], sgu_w_s[0], sgu_b[0],
                              dv=kr // RET_HEADS)

    proj1 = _inproj(h1, jnp.swapaxes(od_w_in, 1, 2), n_main, w_transposed=True)
    w_hi, w_lo = _split_bf16(gla_w_lr[0])
    wlr3 = _pad_rows(jnp.concatenate([w_hi, w_hi, w_lo], axis=0), LANES)
    attn = _gla(proj1.reshape(b, s, -1), lr.reshape(b, s, LANES), wlr3, gla_b_lr[0].reshape(1, -1))
    dk_gla = n_main // 6 // GLA_HEADS
    out = _outproj_final(attn.reshape(t, -1), proj1, x1, od_w_out[0].astype(BF16), final_norm,
                         dv=2 * dk_gla, head_eps=EPS * dk_gla)
    return out.reshape(b, s, d)
```

```python
import functools

import numpy as np
import jax
import jax.numpy as jnp
from jax import lax
from jax.experimental import pallas as pl
from jax.experimental.pallas import tpu as pltpu

F32 = jnp.float32
BF16 = jnp.bfloat16

EPS = 1e-6
RET_HEADS = 4
RET_BLOCK = 256
ROPE_BASE = 10000.0
SGU_GROUPS = 8
SGU_CHUNK = 128
GLA_HEADS = 8
GLA_RANK = 16
GLA_TAU = 16.0
GLA_BLOCK = 128
GLA_SAFE_LOG_DECAY = 60.0

LANES = 128
VMEM_LIMIT = 56 * 1024 * 1024

NORM_TM = 512
PROJ_TM = 1024
PROJ_TN = 1024
OUT_TM = 256
RET_TS = 1024
MID_TM = 256
GLA_TS = 2048


def _params(*sem):
    return pltpu.CompilerParams(dimension_semantics=sem, vmem_limit_bytes=VMEM_LIMIT)


def _silu(g):
    return g * (1.0 / (1.0 + jnp.exp(-g)))


def _rms(x, gain):
    return x * lax.rsqrt(jnp.mean(x * x, axis=-1, keepdims=True) + EPS) * gain


def _norm_proj_kernel(x_ref, g_ref, w_ref, pos_ref, inv_ref, h_ref, o_ref, cos_ref, sin_ref, wbf_ref):
    @pl.when(pl.program_id(0) == 0)
    def _():
        wbf_ref[...] = w_ref[...].astype(BF16)

    h = _rms(x_ref[...], g_ref[...]).astype(BF16)
    h_ref[...] = h
    o_ref[...] = jnp.dot(h, wbf_ref[...], preferred_element_type=F32).astype(o_ref.dtype)
    ang = pos_ref[...] * inv_ref[...]
    cos_ref[...] = jnp.cos(ang)
    sin_ref[...] = jnp.sin(ang)


def _norm_proj(x, gain, w, n_cols, posf, inv):
    t, d = x.shape
    half = inv.shape[1]
    tm = min(NORM_TM, t)
    last = w.shape[2] // n_cols - 1
    row = lambda w_: pl.BlockSpec((tm, w_), lambda i: (i, 0))
    return pl.pallas_call(
        _norm_proj_kernel,
        out_shape=(jax.ShapeDtypeStruct((t, d), BF16), jax.ShapeDtypeStruct((t, n_cols), BF16),
                   jax.ShapeDtypeStruct((t, half), F32), jax.ShapeDtypeStruct((t, half), F32)),
        grid=(t // tm,),
        in_specs=[row(d),
                  pl.BlockSpec((1, d), lambda i: (0, 0)),
                  pl.BlockSpec((None, d, n_cols), lambda i: (0, 0, last), pipeline_mode=pl.Buffered(1)),
                  row(1),
                  pl.BlockSpec((1, half), lambda i: (0, 0))],
        out_specs=(row(d), row(n_cols), row(half), row(half)),
        scratch_shapes=[pltpu.VMEM((d, n_cols), BF16)],
        compiler_params=_params("arbitrary"),
        name="norm_proj",
    )(x, gain.reshape(1, d), w, posf, inv)


def _inproj_kernel(h_ref, w_ref, o_ref, wbf_ref, *, w_transposed):
    @pl.when(pl.program_id(1) == 0)
    def _():
        wbf_ref[...] = w_ref[...].astype(BF16)

    contract_w = 1 if w_transposed else 0
    o_ref[...] = lax.dot_general(h_ref[...], wbf_ref[...], (((1,), (contract_w,)), ((), ())),
                                 preferred_element_type=F32).astype(o_ref.dtype)


def _inproj(h, w, n, w_transposed=False):
    t, d = h.shape
    tm, tn = min(PROJ_TM, t), min(PROJ_TN, n)
    if w_transposed:
        w_spec = pl.BlockSpec((None, tn, d), lambda j, i: (0, j, 0))
    else:
        w_spec = pl.BlockSpec((None, d, tn), lambda j, i: (0, 0, j))
    return pl.pallas_call(
        functools.partial(_inproj_kernel, w_transposed=w_transposed),
        out_shape=jax.ShapeDtypeStruct((t, n), BF16),
        grid=(n // tn, t // tm),
        in_specs=[pl.BlockSpec((tm, d), lambda j, i: (i, 0)), w_spec],
        out_specs=pl.BlockSpec((tm, tn), lambda j, i: (i, j)),
        scratch_shapes=[pltpu.VMEM((tn, d) if w_transposed else (d, tn), BF16)],
        compiler_params=_params("parallel", "arbitrary"),
        name="inproj",
    )(h, w)


def _retention_tables(dk):
    c = RET_BLOCK
    h = np.arange(RET_HEADS, dtype=np.float64)
    log_gamma = np.log1p(-np.power(2.0, -5.0 - h))
    idx = np.arange(c, dtype=np.float64)
    rel = idx[:, None] - idx[None, :]
    dmat = np.where(rel >= 0, np.exp(log_gamma[:, None, None] * np.maximum(rel, 0.0)), 0.0)
    qdec = np.exp(log_gamma[:, None] * (idx + 1.0))
    kdec = np.exp(log_gamma[:, None] * (c - 1.0 - idx))
    cdec = np.exp(log_gamma * c)
    lanes = lambda a: np.broadcast_to(a[:, :, None], (RET_HEADS, c, dk // 2))
    return (jnp.asarray(dmat, F32), jnp.asarray(lanes(qdec), F32), jnp.asarray(lanes(kdec), F32),
            [float(v) for v in cdec], float(dk) ** -0.5)


def _retention_kernel(q_ref, k_ref, v_ref, cos_ref, sin_ref, dmat_ref, qdec_ref, kdec_ref,
                      o_ref, state_ref, *, cdec, scale, dk, dv):
    c = RET_BLOCK
    half = dk // 2
    ts = q_ref.shape[1]

    @pl.when(pl.program_id(1) == 0)
    def _():
        state_ref[...] = jnp.zeros_like(state_ref)

    def chunk(ci, carry):
        rows = pl.ds(pl.multiple_of(ci * c, c), c)
        cos, sin = cos_ref[0, rows, :], sin_ref[0, rows, :]
        cos_q, sin_q = cos * scale, sin * scale
        for h in range(RET_HEADS):
            q = q_ref[0, rows, h * dk:(h + 1) * dk].astype(F32)
            k = k_ref[0, rows, h * dk:(h + 1) * dk].astype(F32)
            q1, q2 = q[:, :half], q[:, half:]
            k1, k2 = k[:, :half], k[:, half:]
            qr1, qr2 = q1 * cos_q - q2 * sin_q, q1 * sin_q + q2 * cos_q
            kr1, kr2 = k1 * cos - k2 * sin, k1 * sin + k2 * cos
            qdec, kdec = qdec_ref[h], kdec_ref[h]
            qr = jnp.concatenate([qr1, qr2], axis=-1).astype(BF16)
            kr = jnp.concatenate([kr1, kr2], axis=-1).astype(BF16)
            qd = jnp.concatenate([qr1 * qdec, qr2 * qdec], axis=-1).astype(BF16)
            kd = jnp.concatenate([kr1 * kdec, kr2 * kdec], axis=-1).astype(BF16)
            s = lax.dot_general(qr, kr, (((1,), (1,)), ((), ())), preferred_element_type=F32)
            p = (s * dmat_ref[h]).astype(BF16)
            v = v_ref[0, rows, h * dv:(h + 1) * dv]
            st = state_ref[h]
            o = (jnp.dot(p, v, preferred_element_type=F32)
                 + jnp.dot(qd, st.astype(BF16), preferred_element_type=F32))
            state_ref[h] = cdec[h] * st + lax.dot_general(
                kd, v, (((0,), (0,)), ((), ())), preferred_element_type=F32)
            o_ref[0, rows, h * dv:(h + 1) * dv] = o.astype(o_ref.dtype)
        return carry

    lax.fori_loop(0, ts // c, chunk, 0)


def _retention(proj, cos, sin, qk_w):
    b, s, _ = proj.shape
    v_w = 2 * qk_w
    dk, dv = qk_w // RET_HEADS, v_w // RET_HEADS
    ts = min(RET_TS, s)
    dmat, qdec, kdec, cdec, scale = _retention_tables(dk)
    const3 = lambda shape: pl.BlockSpec(shape, lambda bi, i: (0, 0, 0))
    return pl.pallas_call(
        functools.partial(_retention_kernel, cdec=cdec, scale=scale, dk=dk, dv=dv),
        out_shape=jax.ShapeDtypeStruct((b, s, v_w), BF16),
        grid=(b, s // ts),
        in_specs=[pl.BlockSpec((1, ts, qk_w), lambda bi, i: (bi, i, 0)),
                  pl.BlockSpec((1, ts, qk_w), lambda bi, i: (bi, i, 1)),
                  pl.BlockSpec((1, ts, v_w), lambda bi, i: (bi, i, 1)),
                  pl.BlockSpec((1, ts, dk // 2), lambda bi, i: (bi, i, 0)),
                  pl.BlockSpec((1, ts, dk // 2), lambda bi, i: (bi, i, 0)),
                  const3(dmat.shape), const3(qdec.shape), const3(kdec.shape)],
        out_specs=pl.BlockSpec((1, ts, v_w), lambda bi, i: (bi, i, 0)),
        scratch_shapes=[pltpu.VMEM((RET_HEADS, dk, dv), F32)],
        compiler_params=_params("parallel", "arbitrary"),
        name="retention",
    )(proj, proj, proj, cos, sin, dmat, qdec, kdec)


def _sgu_block(u_ref, v_ref, g_ref, gain_ref, ws_ref, sb_ref, sgu_ref, *, dg):
    c = SGU_CHUNK
    row = lax.broadcasted_iota(jnp.int32, (c, c), 0)
    col = lax.broadcasted_iota(jnp.int32, (c, c), 1)
    causal = row >= col
    wm = [jnp.where(causal, ws_ref[g], 0.0).astype(BF16) for g in range(SGU_GROUPS)]
    for ci in range(u_ref.shape[0] // c):
        rows = slice(ci * c, (ci + 1) * c)
        for g in range(SGU_GROUPS):
            cols = slice(g * dg, (g + 1) * dg)
            v = v_ref[rows, cols].astype(F32)
            d = v - jnp.mean(v, axis=-1, keepdims=True)
            var = jnp.mean(d * d, axis=-1, keepdims=True)
            vn = d * lax.rsqrt(var + EPS) * gain_ref[:, cols]
            sg = jnp.dot(wm[g], vn.astype(BF16), preferred_element_type=F32) + sb_ref[g]
            sgu_ref[rows, cols] = sg.astype(BF16) * (u_ref[rows, cols] * _silu(g_ref[rows, cols]))


def _gated_heads_matmul(acc, a_ref, g_ref, w_ref, dv, head_eps):
    for h in range(a_ref.shape[1] // dv):
        cols = slice(h * dv, (h + 1) * dv)
        o = a_ref[:, cols].astype(F32)
        on = o * lax.rsqrt(jnp.mean(o * o, axis=-1, keepdims=True) + head_eps)
        mixed = on.astype(BF16) * _silu(g_ref[:, cols])
        acc = acc + jnp.dot(mixed, w_ref[cols, :], preferred_element_type=F32)
    return acc


def _outproj_mid_kernel(a_ref, ga_ref, u_ref, v_ref, g_ref, x_ref, wa_ref, wb_ref, gn_ref, wlr_ref,
                        sgain_ref, ws_ref, sb_ref, x1_ref, h1_ref, lr_ref, sgu_ref, *, dg, dv):
    acc = _gated_heads_matmul(x_ref[...], a_ref, ga_ref, wa_ref, dv, EPS)
    _sgu_block(u_ref, v_ref, g_ref, sgain_ref, ws_ref, sb_ref, sgu_ref, dg=dg)
    x1 = acc + jnp.dot(sgu_ref[...], wb_ref[...], preferred_element_type=F32)
    x1_ref[...] = x1
    h1 = _rms(x1, gn_ref[...]).astype(h1_ref.dtype)
    h1_ref[...] = h1
    lr_ref[...] = jnp.dot(h1, wlr_ref[...], preferred_element_type=F32)


def _outproj_final_kernel(a_ref, g_ref, x_ref, w_ref, gn_ref, o_ref, *, dv, head_eps):
    x1 = _gated_heads_matmul(x_ref[...], a_ref, g_ref, w_ref, dv, head_eps)
    o_ref[...] = _rms(x1, gn_ref[...])


def _resident(shape):
    return pl.BlockSpec(shape, lambda i: (0, 0), pipeline_mode=pl.Buffered(1))


def _outproj_mid(a, proj, g_s, x, w, gain, wlr, sgu_gain, w_s, b_s, dv):
    t, d = x.shape
    ka = a.shape[1]
    kb = g_s.shape[1]
    assert ka == kb and w.shape == (ka + kb, d)
    dg = kb // SGU_GROUPS
    tm = min(MID_TM, t)
    bias = jnp.broadcast_to(b_s[:, :, None], (SGU_GROUPS, SGU_CHUNK, dg))
    row = lambda w_, blk=0: pl.BlockSpec((tm, w_), lambda i: (i, blk))
    w_half = lambda blk: pl.BlockSpec((ka, d), lambda i: (blk, 0), pipeline_mode=pl.Buffered(1))
    const3 = lambda shape: pl.BlockSpec(shape, lambda i: (0, 0, 0), pipeline_mode=pl.Buffered(1))
    return pl.pallas_call(
        functools.partial(_outproj_mid_kernel, dg=dg, dv=dv),
        out_shape=(jax.ShapeDtypeStruct((t, d), F32), jax.ShapeDtypeStruct((t, d), BF16),
                   jax.ShapeDtypeStruct((t, LANES), F32)),
        grid=(t // tm,),
        in_specs=[row(ka), row(ka, 2), row(kb, 3), row(kb, 4), row(kb), row(d), w_half(0), w_half(1),
                  _resident((1, d)), _resident((d, LANES)), _resident((1, kb)),
                  const3(w_s.shape), const3(bias.shape)],
        out_specs=(row(d), row(d), row(LANES)),
        scratch_shapes=[pltpu.VMEM((tm, kb), BF16)],
        compiler_params=_params("parallel"),
        name="outproj_mid",
    )(a, proj, proj, proj, g_s, x, w, w, gain.reshape(1, d), wlr, sgu_gain.reshape(1, kb), w_s, bias)


def _outproj_final(a, proj, x, w, gain, dv, head_eps):
    t, d = x.shape
    ka = a.shape[1]
    tm = min(OUT_TM, t)
    row = lambda w_, blk=0: pl.BlockSpec((tm, w_), lambda i: (i, blk))
    return pl.pallas_call(
        functools.partial(_outproj_final_kernel, dv=dv, head_eps=head_eps),
        out_shape=jax.ShapeDtypeStruct((t, d), F32),
        grid=(t // tm,),
        in_specs=[row(ka), row(ka, 2), row(d), _resident((ka, d)), _resident((1, d))],
        out_specs=row(d),
        compiler_params=_params("parallel"),
        name="outproj_final",
    )(a, proj, x, w, gain.reshape(1, d))


def _split_bf16(x):
    hi = x.astype(BF16)
    lo = (x - hi.astype(F32)).astype(BF16)
    return hi, lo


def _gla_kernel(q_ref, k_ref, v_ref, lr_ref, wlr_ref, blr_ref, o_ref,
                state_ref, state0_ref, bc_ref, kf_ref):
    c = GLA_BLOCK
    ts = q_ref.shape[1]
    n_blocks = ts // c

    @pl.when(pl.program_id(2) == 0)
    def _():
        state_ref[...] = jnp.zeros_like(state_ref)

    state0_ref[...] = state_ref[...]

    lr_hi, lr_lo = _split_bf16(lr_ref[0])
    lane = lax.broadcasted_iota(jnp.int32, lr_hi.shape, 1)
    lr3 = jnp.where((lane >= GLA_RANK) & (lane < 2 * GLA_RANK), lr_lo, lr_hi)
    z = jnp.dot(lr3, wlr_ref[...], preferred_element_type=F32) + blr_ref[...]
    log_alpha = (jnp.minimum(z, 0.0) - jnp.log(1.0 + jnp.exp(-jnp.abs(z)))) * (1.0 / GLA_TAU)

    row = lax.broadcasted_iota(jnp.int32, (c, c), 0)
    col = lax.broadcasted_iota(jnp.int32, (c, c), 1)
    causal = row >= col
    tri = jnp.where(causal, 1.0, 0.0).astype(BF16)
    tri2 = jnp.concatenate([tri, tri], axis=1)

    worst = None
    for ci in range(n_blocks):
        rows = slice(ci * c, (ci + 1) * c)
        la_hi, la_lo = _split_bf16(log_alpha[rows])
        bc = jnp.dot(tri2, jnp.concatenate([la_hi, la_lo], axis=0), preferred_element_type=F32)
        bc_ref[rows, :] = bc
        worst = bc[c - 1:c, :] if worst is None else jnp.minimum(worst, bc[c - 1:c, :])

    def factored_scores(ci, bc, q, k, q_dec, k_dec):
        mid = bc[c // 2 - 1:c // 2, :]
        last = bc[c - 1:c, :]
        q_t = (q_dec * jnp.exp(-mid)).astype(BF16)
        k_t = (k_dec * jnp.exp(mid - last)).astype(BF16)
        s = lax.dot_general(q_t, k_t, (((1,), (1,)), ((), ())), preferred_element_type=F32)
        return jnp.where(causal, s, 0.0).astype(BF16)

    def exact_scores(ci, bc, q, k, q_dec, k_dec):
        kf_ref[...] = k

        def column(s, p):
            w = jnp.exp(jnp.minimum(bc - bc_ref[pl.ds(ci * c + s, 1), :], 0.0))
            val = jnp.sum(q * w * kf_ref[pl.ds(s, 1), :], axis=-1, keepdims=True)
            return jnp.where(col == s, val, p)

        p = lax.fori_loop(0, c, column, jnp.zeros((c, c), F32))
        return jnp.where(causal, p, 0.0).astype(BF16)

    def run_blocks(scores_fn):
        local = []
        for ci in range(n_blocks):
            rows = slice(ci * c, (ci + 1) * c)
            bc = bc_ref[rows, :]
            last = bc[c - 1:c, :]
            q = q_ref[0, rows, :].astype(F32)
            k = k_ref[0, rows, :].astype(F32)
            v = v_ref[0, rows, :]
            q_dec = q * jnp.exp(bc)
            k_dec = k * jnp.exp(last - bc)
            p = scores_fn(ci, bc, q, k, q_dec, k_dec)
            intra = jnp.dot(p, v, preferred_element_type=F32)
            update = lax.dot_general(v, k_dec.astype(BF16), (((0,), (0,)), ((), ())),
                                     preferred_element_type=F32)
            local.append((q_dec.astype(BF16), intra, update, jnp.exp(last)))
        for ci, (q_in, intra, update, decay) in enumerate(local):
            rows = slice(ci * c, (ci + 1) * c)
            st = state_ref[...]
            o = intra + lax.dot_general(q_in, st.astype(BF16), (((1,), (1,)), ((), ())),
                                        preferred_element_type=F32)
            state_ref[...] = st * decay + update
            o_ref[0, rows, :] = o.astype(o_ref.dtype)

    run_blocks(factored_scores)

    @pl.when(jnp.logical_not(jnp.min(worst) > -GLA_SAFE_LOG_DECAY))
    def _():
        state_ref[...] = state0_ref[...]
        run_blocks(exact_scores)


def _gla(proj, lr, wlr3, blr):
    b, s, _ = proj.shape
    qk_w = proj.shape[2] // 6
    v_w = 2 * qk_w
    dk, dv = qk_w // GLA_HEADS, v_w // GLA_HEADS
    ts = min(GLA_TS, s)
    nh = GLA_HEADS
    return pl.pallas_call(
        _gla_kernel,
        out_shape=jax.ShapeDtypeStruct((b, s, v_w), BF16),
        grid=(b, nh, s // ts),
        in_specs=[pl.BlockSpec((1, ts, dk), lambda bi, h, i: (bi, i, h)),
                  pl.BlockSpec((1, ts, dk), lambda bi, h, i: (bi, i, nh + h)),
                  pl.BlockSpec((1, ts, dv), lambda bi, h, i: (bi, i, nh + h)),
                  pl.BlockSpec((1, ts, LANES), lambda bi, h, i: (bi, i, 0)),
                  pl.BlockSpec((LANES, dk), lambda bi, h, i: (0, h)),
                  pl.BlockSpec((1, dk), lambda bi, h, i: (0, h))],
        out_specs=pl.BlockSpec((1, ts, dv), lambda bi, h, i: (bi, i, h)),
        scratch_shapes=[pltpu.VMEM((dv, dk), F32),
                        pltpu.VMEM((dv, dk), F32),
                        pltpu.VMEM((ts, dk), F32),
                        pltpu.VMEM((GLA_BLOCK, dk), F32)],
        compiler_params=_params("parallel", "parallel", "arbitrary"),
        name="gla",
    )(proj, proj, proj, lr, wlr3, blr)


def _pad_rows(a, rows):
    return jnp.pad(a, ((0, rows - a.shape[0]), (0, 0)))


def kernel(x, positions, ev_norm, ev_w_in, sgu_gain, sgu_w_s, sgu_b, ev_w_out, od_norm, od_w_in,
           gla_w_lr, gla_b_lr, od_w_out, final_norm):
    b, s, d = x.shape
    t = b * s
    x2 = x.reshape(t, d)

    qk_w = ev_w_in.shape[2] // 12
    half = qk_w // RET_HEADS // 2
    inv = jnp.power(ROPE_BASE, -jnp.arange(half, dtype=F32) / half).reshape(1, half)
    posf = positions.astype(F32).reshape(t, 1)
    h0, g_s, cos, sin = _norm_proj(x2, ev_norm[0], ev_w_in, 2 * qk_w, posf, inv)
    proj0 = _inproj(h0, ev_w_in, 10 * qk_w).reshape(b, s, -1)
    ret = _retention(proj0, cos.reshape(b, s, half), sin.reshape(b, s, half), qk_w)
    w_out0 = ev_w_out[0].astype(BF16)
    kr = ret.shape[2]
    n_main = od_w_in.shape[2] - GLA_RANK
    w_lr_in = jnp.pad(jnp.tile(od_w_in[0, :, n_main:], (1, 3)), ((0, 0), (0, LANES - 3 * GLA_RANK)))
    x1, h1, lr = _outproj_mid(ret.reshape(t, kr), proj0.reshape(t, -1), g_s, x2, w_out0, od_norm[0],
                              w_lr_in.astype(BF16), sgu_gain[0], sgu_w_s[0], sgu_b[0],
                              dv=kr // RET_HEADS)

    proj1 = _inproj(h1, jnp.swapaxes(od_w_in, 1, 2), n_main, w_transposed=True)
    w_hi, w_lo = _split_bf16(gla_w_lr[0])
    wlr3 = _pad_rows(jnp.concatenate([w_hi, w_hi, w_lo], axis=0), LANES)
    attn = _gla(proj1.reshape(b, s, -1), lr.reshape(b, s, LANES), wlr3, gla_b_lr[0].reshape(1, -1))
    dk_gla = n_main // 6 // GLA_HEADS
    out = _outproj_final(attn.reshape(t, -1), proj1, x1, od_w_out[0].astype(BF16), final_norm,
                         dv=2 * dk_gla, head_eps=EPS * dk_gla)
    return out.reshape(b, s, d)
```
